```python
import math
import jax
import jax.numpy as jnp
from jax import lax
import numpy as np

D_MODEL = 2048
BATCH = 4
SEQ = 4096
DEPTH = 4

N_MIXERS = 4
EPS = 1e-6
ROPE_THETA = 10000.0
NEG_INF = -1e30
SEL_BIG = 1e9

W_TOK = 2 * D_MODEL
MEM_TOKENS = 256
MEM_HEADS = 4
MEM_HEAD_DIM = D_MODEL // 8
W_MEM = MEM_HEADS * MEM_HEAD_DIM

POOL_WINDOWS = (2, 4, 8, 16)
POOL_GROUPS = 4
POOL_GW = W_TOK // POOL_GROUPS

NSA_HEADS = 32
NSA_KV_HEADS = 4
NSA_GROUP = NSA_HEADS // NSA_KV_HEADS
NSA_HEAD_DIM = 128
NSA_KV_W = NSA_KV_HEADS * NSA_HEAD_DIM
CMP_BLOCK = 32
CMP_STRIDE = 16
SEL_BLOCK = 64
SEL_TOPK = 16
WINDOW = 512
NSA_QBLOCK = 32

SSM_HEAD_DIM = 64
SSM_HEADS = W_TOK // SSM_HEAD_DIM
SSM_GROUPS = 8
SSM_STATE = 128
CONV_W = 4
SSD_CHUNK = 256
SSM_XBC_W = W_TOK + 2 * SSM_GROUPS * SSM_STATE

SGU_CHUNK = 128
SGU_GROUPS = 8
SGU_GW = W_TOK // SGU_GROUPS

POS_OFFSET_MAX = 2048

TOK_COLS = (2 * W_TOK,
            2 * W_TOK + 6 * NSA_KV_W + 3 * NSA_HEADS,
            W_TOK + SSM_XBC_W + SSM_HEADS,
            3 * W_TOK)

kernel_name = 'hybrid_interleaved_pool_nsa_ssd_sgu_memxattn'


def rms_norm(x, g):
    xf = x.astype(jnp.float32)
    y = xf * lax.rsqrt(jnp.mean(jnp.square(xf), axis=-1, keepdims=True) + EPS)
    return (y * g.astype(jnp.float32)).astype(x.dtype)


def masked_softmax(scores, mask):
    return jax.nn.softmax(jnp.where(mask, scores.astype(jnp.float32), NEG_INF), axis=-1)


def rope_tables(positions, dim):
    inv = ROPE_THETA ** (-jnp.arange(0, dim, 2, dtype=jnp.float32) / dim)
    ang = positions.astype(jnp.float32)[..., None] * inv
    return jnp.cos(ang), jnp.sin(ang)


def apply_rope(x, cos, sin):
    x1, x2 = jnp.split(x.astype(jnp.float32), 2, axis=-1)
    c = cos[:, :, None, :]
    s = sin[:, :, None, :]
    return jnp.concatenate([x1 * c - x2 * s, x2 * c + x1 * s], axis=-1).astype(x.dtype)


def causal_depthwise_conv(x, w, b):
    c = x.shape[-1]
    y = lax.conv_general_dilated(x, w[:, None, :].astype(x.dtype), window_strides=(1,),
                                 padding=[(CONV_W - 1, 0)],
                                 dimension_numbers=('NWC', 'WIO', 'NWC'),
                                 feature_group_count=c)
    return y + b.astype(x.dtype)


def memory_branch(cols, mem_n, mem_wkv, q_norm, k_norm):
    q, gate = jnp.split(cols, 2, axis=-1)
    B_, S_, _ = q.shape
    k, v = jnp.split(jnp.einsum('bmd,de->bme', mem_n, mem_wkv), 2, axis=-1)
    q = rms_norm(q.reshape(B_, S_, MEM_HEADS, MEM_HEAD_DIM), q_norm)
    k = rms_norm(k.reshape(B_, -1, MEM_HEADS, MEM_HEAD_DIM), k_norm)
    v = v.reshape(B_, -1, MEM_HEADS, MEM_HEAD_DIM)
    s = jnp.einsum('bqhd,bmhd->bhqm', q, k) * (MEM_HEAD_DIM ** -0.5)
    p = jax.nn.softmax(s.astype(jnp.float32), axis=-1).astype(v.dtype)
    o = jnp.einsum('bhqm,bmhd->bqhd', p, v).reshape(B_, S_, W_MEM)
    return o * jax.nn.silu(gate)


def pooling_mixer(cols, pool_w, pool_scale):
    v, gate = jnp.split(cols, 2, axis=-1)
    B_, S_, _ = v.shape
    vf = v.astype(jnp.float32).reshape(B_, S_, POOL_GROUPS, POOL_GW)
    cs = jnp.concatenate([jnp.zeros_like(vf[:, :1]), jnp.cumsum(vf, axis=1)], axis=1)
    t = np.arange(S_)
    win = np.array(POOL_WINDOWS)
    lo = np.maximum(t[:, None] + 1 - win[None, :], 0)
    cnt = (t[:, None] + 1 - lo).astype(np.float32)
    window_sum = cs[:, 1:] - cs[:, lo, np.arange(POOL_GROUPS)[None, :]]
    mix = window_sum / cnt[None, :, :, None] - vf
    out = jnp.einsum('bsgc,gcd->bsgd', mix, pool_w.astype(jnp.float32)).reshape(B_, S_, W_TOK)
    out = out * pool_scale.astype(jnp.float32)
    return out.astype(cols.dtype) * jax.nn.silu(gate)


def nsa_mixer(cols, cos, sin, q_norm, k_norm, cmp_pos, cmp_k_w1, cmp_k_w2, cmp_v_w1, cmp_v_w2):
    B_, S_, _ = cols.shape
    HKV, G, d = NSA_KV_HEADS, NSA_GROUP, NSA_HEAD_DIM
    splits = np.cumsum([W_TOK] + [NSA_KV_W] * 6 + [NSA_HEADS * 3]).tolist()
    q, k_c, v_c, k_s, v_s, k_w, v_w, g_logit, gate = jnp.split(cols, splits, axis=-1)
    q = apply_rope(rms_norm(q.reshape(B_, S_, NSA_HEADS, d), q_norm), cos, sin).reshape(B_, S_, HKV, G, d)

    def keys(a):
        return apply_rope(rms_norm(a.reshape(B_, S_, HKV, d), k_norm), cos, sin)

    k_c, k_s, k_w = keys(k_c), keys(k_s), keys(k_w)
    v_c = v_c.reshape(B_, S_, HKV, d)
    v_s = v_s.reshape(B_, S_, HKV, d)
    v_w = v_w.reshape(B_, S_, HKV, d)
    branch_gate = jax.nn.sigmoid(g_logit.astype(jnp.float32)).astype(cols.dtype).reshape(B_, S_, HKV, G, 3)

    n_cmp = (S_ - CMP_BLOCK) // CMP_STRIDE + 1
    cmp_start = np.arange(n_cmp) * CMP_STRIDE
    cmp_idx = cmp_start[:, None] + np.arange(CMP_BLOCK)[None, :]
    cmp_end = cmp_start + CMP_BLOCK - 1

    def compress(a, w1, w2):
        blk = a[:, cmp_idx] + cmp_pos[:, None, :].astype(a.dtype)
        blk = blk.transpose(0, 1, 3, 2, 4).reshape(B_, n_cmp, HKV, CMP_BLOCK * d)
        return jax.nn.silu(blk @ w1) @ w2

    kc = compress(k_c, cmp_k_w1, cmp_k_w2)
    vc = compress(v_c, cmp_v_w1, cmp_v_w2)

    n_sel = S_ // SEL_BLOCK
    top_k = min(SEL_TOPK, n_sel)
    sel_start = np.arange(n_sel) * SEL_BLOCK
    agg = ((cmp_start[:, None] < sel_start[None, :] + SEL_BLOCK)
           & (cmp_end[:, None] >= sel_start[None, :])).astype(np.float32)
    ks_blk = k_s.reshape(B_, n_sel, SEL_BLOCK, HKV, d).transpose(0, 3, 1, 2, 4)
    vs_blk = v_s.reshape(B_, n_sel, SEL_BLOCK, HKV, d).transpose(0, 3, 1, 2, 4)
    b_ix = jnp.arange(B_)[:, None, None, None]
    h_ix = jnp.arange(HKV)[None, :, None, None]
    sel_ids = np.arange(n_sel)

    pad = ((0, 0), (WINDOW, 0), (0, 0), (0, 0))
    k_w_pad = jnp.pad(k_w, pad)
    v_w_pad = jnp.pad(v_w, pad)
    scale = d ** -0.5

    def query_block(s0):
        t = s0 + jnp.arange(NSA_QBLOCK)
        qb = lax.dynamic_slice_in_dim(q, s0, NSA_QBLOCK, axis=1)
        gb = lax.dynamic_slice_in_dim(branch_gate, s0, NSA_QBLOCK, axis=1)
        c_mask = cmp_end[None, :] <= t[:, None]
        p_cmp = masked_softmax(jnp.einsum('bqhgd,bchd->bhgqc', qb, kc) * scale, c_mask)
        p_cmp = p_cmp * jnp.any(c_mask, axis=-1)[:, None]
        o_cmp = jnp.einsum('bhgqc,bchd->bqhgd', p_cmp.astype(vc.dtype), vc)
        imp = jnp.einsum('bhgqc,cj->bhqj', p_cmp, agg)
        bt = (t // SEL_BLOCK)[:, None]
        causal = sel_ids[None, :] <= bt
        forced = (sel_ids[None, :] == 0) | (sel_ids[None, :] == bt) | (sel_ids[None, :] == bt - 1)
        score = jnp.where(forced, SEL_BIG, jnp.where(causal, imp, -SEL_BIG))
        _, sel = lax.top_k(score, top_k)
        kb = ks_blk[b_ix, h_ix, sel]
        vb = vs_blk[b_ix, h_ix, sel]
        pos = sel[..., None] * SEL_BLOCK + np.arange(SEL_BLOCK)
        s_mask = (pos <= t[None, None, :, None, None]).reshape(B_, HKV, 1, NSA_QBLOCK, top_k * SEL_BLOCK)
        s_sel = jnp.einsum('bqhgd,bhqnld->bhgqnl', qb, kb) * scale
        p_sel = masked_softmax(s_sel.reshape(B_, HKV, G, NSA_QBLOCK, top_k * SEL_BLOCK), s_mask)
        o_sel = jnp.einsum('bhgqk,bhqkd->bqhgd', p_sel.astype(vb.dtype),
                           vb.reshape(B_, HKV, NSA_QBLOCK, top_k * SEL_BLOCK, d))
        kwb = lax.dynamic_slice_in_dim(k_w_pad, s0, WINDOW + NSA_QBLOCK, axis=1)
        vwb = lax.dynamic_slice_in_dim(v_w_pad, s0, WINDOW + NSA_QBLOCK, axis=1)
        kp = s0 - WINDOW + jnp.arange(WINDOW + NSA_QBLOCK)
        w_mask = (kp[None, :] >= 0) & (kp[None, :] <= t[:, None]) & (kp[None, :] > t[:, None] - WINDOW)
        p_win = masked_softmax(jnp.einsum('bqhgd,bkhd->bhgqk', qb, kwb) * scale, w_mask)
        o_win = jnp.einsum('bhgqk,bkhd->bqhgd', p_win.astype(vwb.dtype), vwb)
        return gb[..., 0:1] * o_cmp + gb[..., 1:2] * o_sel + gb[..., 2:3] * o_win

    starts = jnp.arange(S_ // NSA_QBLOCK, dtype=jnp.int32) * NSA_QBLOCK
    o = lax.map(query_block, starts)
    o = o.transpose(1, 0, 2, 3, 4, 5).reshape(B_, S_, W_TOK)
    return o * jax.nn.silu(gate)


def ssd_chunked(x, dt, A, Bm, Cm):
    B_, S_, G_, R_, P_ = x.shape
    L = math.gcd(SSD_CHUNK, S_)
    nc = S_ // L

    def chunks(a):
        a = a.astype(jnp.float32)
        return a.reshape(B_, nc, L, *a.shape[2:]).swapaxes(0, 1)

    tri = np.tril(np.ones((L, L), dtype=bool))[None, :, :, None, None]
    A = A.astype(jnp.float32)

    def step(state, inp):
        xk, dtk, Bk, Ck = inp
        a = jnp.cumsum(dtk * A, axis=1)
        decay = jnp.exp(jnp.where(tri, a[:, :, None] - a[:, None, :], -jnp.inf))
        cb = jnp.einsum('blgn,bsgn->blsg', Ck, Bk)
        m = cb[..., None] * decay * dtk[:, None]
        y_in = jnp.einsum('blsgr,bsgrp->blgrp', m, xk)
        y_st = jnp.einsum('blgn,bgrpn->blgrp', Ck, state) * jnp.exp(a)[..., None]
        a_end = a[:, -1]
        w = jnp.exp(a_end[:, None] - a) * dtk
        state = state * jnp.exp(a_end)[..., None, None] + jnp.einsum('blgn,blgr,blgrp->bgrpn', Bk, w, xk)
        return state, y_in + y_st

    state0 = jnp.zeros((B_, G_, R_, P_, SSM_STATE), jnp.float32)
    _, y = lax.scan(step, state0, (chunks(x), chunks(dt), chunks(Bm), chunks(Cm)))
    return y.swapaxes(0, 1).reshape(B_, S_, G_, R_, P_)


def mamba2_mixer(cols, conv_w, conv_b, dt_bias, A_log, D_skip, gnorm):
    B_, S_, _ = cols.shape
    R = SSM_HEADS // SSM_GROUPS
    z, xbc, dt = jnp.split(cols, [W_TOK, W_TOK + SSM_XBC_W], axis=-1)
    xbc = jax.nn.silu(causal_depthwise_conv(xbc, conv_w, conv_b))
    xs, Bm, Cm = jnp.split(xbc, [W_TOK, W_TOK + SSM_GROUPS * SSM_STATE], axis=-1)
    dt = jax.nn.softplus(dt.astype(jnp.float32) + dt_bias.astype(jnp.float32))
    A = -jnp.exp(A_log.astype(jnp.float32))
    x5 = xs.reshape(B_, S_, SSM_GROUPS, R, SSM_HEAD_DIM)
    y = ssd_chunked(x5, dt.reshape(B_, S_, SSM_GROUPS, R), A.reshape(SSM_GROUPS, R),
                    Bm.reshape(B_, S_, SSM_GROUPS, SSM_STATE), Cm.reshape(B_, S_, SSM_GROUPS, SSM_STATE))
    y = y + D_skip.astype(jnp.float32).reshape(SSM_GROUPS, R, 1) * x5.astype(jnp.float32)
    y = y.reshape(B_, S_, W_TOK).astype(cols.dtype)
    return rms_norm(y * jax.nn.silu(z), gnorm)


def sgu_mixer(cols, v_norm, sgu_w, sgu_b):
    B_, S_, _ = cols.shape
    u, v, gate = jnp.split(cols, 3, axis=-1)
    u = jax.nn.gelu(u, approximate=False)
    v = rms_norm(jax.nn.gelu(v, approximate=False), v_norm)
    vv = v.reshape(B_, S_ // SGU_CHUNK, SGU_CHUNK, SGU_GROUPS, SGU_GW)
    w = (sgu_w * np.tril(np.ones((SGU_CHUNK, SGU_CHUNK), np.float32))).astype(v.dtype)
    mixed = jnp.einsum('gts,bcsgd->bctgd', w, vv) + sgu_b.T[:, :, None].astype(v.dtype)
    return (u * mixed.reshape(B_, S_, W_TOK)) * jax.nn.silu(gate)


def setup_inputs(seed: int = 0) -> dict:
    key = jax.random.key(seed)
    keys = list(jax.random.split(key, 80))

    def nxt():
        return keys.pop()

    def dense(shape, fan_in):
        return jax.random.normal(nxt(), shape, jnp.float32) * (fan_in ** -0.5)

    def gain(n):
        return 1.0 + 0.02 * jax.random.normal(nxt(), (n,), jnp.float32)

    inp = {}
    inp['x'] = jax.random.normal(nxt(), (BATCH, SEQ, D_MODEL), jnp.float32)
    inp['mem'] = jax.random.normal(nxt(), (BATCH, MEM_TOKENS, D_MODEL), jnp.float32)
    offset = jax.random.randint(nxt(), (BATCH, 1), 0, POS_OFFSET_MAX, dtype=jnp.int32)
    inp['positions'] = jnp.arange(SEQ, dtype=jnp.int32)[None, :] + offset
    inp['mem_norm'] = gain(D_MODEL)
    for i in range(DEPTH):
        p = f'l{i}_'
        m = i % N_MIXERS
        inp[p + 'norm'] = gain(D_MODEL)
        inp[p + 'w_in'] = dense((D_MODEL, TOK_COLS[m] + 2 * W_MEM), D_MODEL)
        inp[p + 'w_out'] = dense((W_TOK + W_MEM, D_MODEL), W_TOK + W_MEM)
        inp[p + 'mem_wkv'] = dense((D_MODEL, 2 * W_MEM), D_MODEL)
        inp[p + 'mem_qnorm'] = gain(MEM_HEAD_DIM)
        inp[p + 'mem_knorm'] = gain(MEM_HEAD_DIM)
        if m == 0:
            inp[p + 'pool_w'] = dense((POOL_GROUPS, POOL_GW, POOL_GW), POOL_GW)
            inp[p + 'pool_scale'] = gain(W_TOK)
        elif m == 1:
            inp[p + 'qnorm'] = gain(NSA_HEAD_DIM)
            inp[p + 'knorm'] = gain(NSA_HEAD_DIM)
            inp[p + 'cmp_pos'] = 0.1 * jax.random.normal(nxt(), (CMP_BLOCK, NSA_HEAD_DIM), jnp.float32)
            inp[p + 'cmp_k_w1'] = dense((CMP_BLOCK * NSA_HEAD_DIM, NSA_HEAD_DIM), CMP_BLOCK * NSA_HEAD_DIM)
            inp[p + 'cmp_k_w2'] = dense((NSA_HEAD_DIM, NSA_HEAD_DIM), NSA_HEAD_DIM)
            inp[p + 'cmp_v_w1'] = dense((CMP_BLOCK * NSA_HEAD_DIM, NSA_HEAD_DIM), CMP_BLOCK * NSA_HEAD_DIM)
            inp[p + 'cmp_v_w2'] = dense((NSA_HEAD_DIM, NSA_HEAD_DIM), NSA_HEAD_DIM)
        elif m == 2:
            inp[p + 'conv_w'] = dense((CONV_W, SSM_XBC_W), CONV_W)
            inp[p + 'conv_b'] = 0.01 * jax.random.normal(nxt(), (SSM_XBC_W,), jnp.float32)
            dt0 = jnp.exp(jax.random.uniform(nxt(), (SSM_HEADS,), jnp.float32,
                                             minval=math.log(1e-3), maxval=math.log(1e-1)))
            inp[p + 'dt_bias'] = dt0 + jnp.log(-jnp.expm1(-dt0))
            inp[p + 'A_log'] = jnp.log(jax.random.uniform(nxt(), (SSM_HEADS,), jnp.float32, minval=1.0, maxval=16.0))
            inp[p + 'D'] = gain(SSM_HEADS)
            inp[p + 'gnorm'] = gain(W_TOK)
        else:
            inp[p + 'v_norm'] = gain(W_TOK)
            inp[p + 'sgu_w'] = dense((SGU_GROUPS, SGU_CHUNK, SGU_CHUNK), SGU_CHUNK)
            inp[p + 'sgu_b'] = 1.0 + 0.02 * jax.random.normal(nxt(), (SGU_GROUPS, SGU_CHUNK), jnp.float32)
    return inp


def reference(x, mem, positions, mem_norm,
              l0_norm, l0_w_in, l0_w_out, l0_mem_wkv, l0_mem_qnorm, l0_mem_knorm, l0_pool_w, l0_pool_scale,
              l1_norm, l1_w_in, l1_w_out, l1_mem_wkv, l1_mem_qnorm, l1_mem_knorm,
              l1_qnorm, l1_knorm, l1_cmp_pos, l1_cmp_k_w1, l1_cmp_k_w2, l1_cmp_v_w1, l1_cmp_v_w2,
              l2_norm, l2_w_in, l2_w_out, l2_mem_wkv, l2_mem_qnorm, l2_mem_knorm,
              l2_conv_w, l2_conv_b, l2_dt_bias, l2_A_log, l2_D, l2_gnorm,
              l3_norm, l3_w_in, l3_w_out, l3_mem_wkv, l3_mem_qnorm, l3_mem_knorm,
              l3_v_norm, l3_sgu_w, l3_sgu_b):
    cos, sin = rope_tables(positions, NSA_HEAD_DIM)
    mem_n = rms_norm(mem, mem_norm)
    token_mixers = (
        lambda c: pooling_mixer(c, l0_pool_w, l0_pool_scale),
        lambda c: nsa_mixer(c, cos, sin, l1_qnorm, l1_knorm, l1_cmp_pos,
                            l1_cmp_k_w1, l1_cmp_k_w2, l1_cmp_v_w1, l1_cmp_v_w2),
        lambda c: mamba2_mixer(c, l2_conv_w, l2_conv_b, l2_dt_bias, l2_A_log, l2_D, l2_gnorm),
        lambda c: sgu_mixer(c, l3_v_norm, l3_sgu_w, l3_sgu_b),
    )
    layer_io = (
        (l0_norm, l0_w_in, l0_w_out, l0_mem_wkv, l0_mem_qnorm, l0_mem_knorm),
        (l1_norm, l1_w_in, l1_w_out, l1_mem_wkv, l1_mem_qnorm, l1_mem_knorm),
        (l2_norm, l2_w_in, l2_w_out, l2_mem_wkv, l2_mem_qnorm, l2_mem_knorm),
        (l3_norm, l3_w_in, l3_w_out, l3_mem_wkv, l3_mem_qnorm, l3_mem_knorm),
    )
    for i in range(DEPTH):
        norm, w_in, w_out, mem_wkv, mem_qn, mem_kn = layer_io[i]
        h = rms_norm(x, norm)
        proj = jnp.einsum('bsd,de->bse', h, w_in)
        y_tok = token_mixers[i % N_MIXERS](proj[..., :-2 * W_MEM])
        y_mem = memory_branch(proj[..., -2 * W_MEM:], mem_n, mem_wkv, mem_qn, mem_kn)
        x = x + jnp.einsum('bse,ed->bsd', jnp.concatenate([y_tok, y_mem], axis=-1), w_out)
    return x
```

```python
import functools
import math

import numpy as np
import jax
import jax.numpy as jnp
from jax import lax
from jax.experimental import pallas as pl
from jax.experimental.pallas import tpu as pltpu

F32 = jnp.float32
BF16 = jnp.bfloat16

D_MODEL = 2048
EPS = 1e-6
ROPE_THETA = 10000.0
NEG_INF = -1e30
SEL_BIG = 1e9

W_TOK = 2 * D_MODEL
MEM_TOKENS = 256
MEM_HEADS = 4
MEM_HEAD_DIM = D_MODEL // 8
W_MEM = MEM_HEADS * MEM_HEAD_DIM

POOL_GROUPS = 4
POOL_GW = W_TOK // POOL_GROUPS
POOL_HALO = 16

NSA_HEADS = 32
NSA_KV_HEADS = 4
NSA_GROUP = NSA_HEADS // NSA_KV_HEADS
NSA_HEAD_DIM = 128
NSA_KV_W = NSA_KV_HEADS * NSA_HEAD_DIM
CMP_BLOCK = 32
CMP_STRIDE = 16
SEL_BLOCK = 64
SEL_TOPK = 16
WINDOW = 512

SSM_HEAD_DIM = 64
SSM_HEADS = W_TOK // SSM_HEAD_DIM
SSM_GROUPS = 8
SSM_RANK = SSM_HEADS // SSM_GROUPS
SSM_STATE = 128
CONV_W = 4
SSD_CHUNK = 256
SSM_GX = W_TOK // SSM_GROUPS

SGU_CHUNK = 128
SGU_GROUPS = 8
SGU_GW = W_TOK // SGU_GROUPS

LANES = 128
VMEM_LIMIT = 56 * 1024 * 1024


def _cparams(sem):
    return pltpu.CompilerParams(dimension_semantics=sem, vmem_limit_bytes=VMEM_LIMIT)


def _silu(x):
    return x * jax.nn.sigmoid(x)


def _gelu_exact(x):
    return 0.5 * x * (1.0 + lax.erf(x * (1.0 / math.sqrt(2.0))))


def _dot(a, b):
    return jnp.dot(a, b, preferred_element_type=F32)


def _dot_nt(a, b):
    return lax.dot_general(a, b, (((1,), (1,)), ((), ())), preferred_element_type=F32)


def _split3(x):
    hi = x.astype(BF16)
    r1 = x - hi.astype(F32)
    mid = r1.astype(BF16)
    lo = (r1 - mid.astype(F32)).astype(BF16)
    return hi, mid, lo


def _norm_matmul_kernel(x_ref, g_ref, w_ref, o_ref, h_ref):
    @pl.when(pl.program_id(1) == 0)
    def _():
        x = x_ref[...].astype(F32)
        ms = jnp.mean(x * x, axis=-1, keepdims=True)
        h_ref[...] = (x * lax.rsqrt(ms + EPS) * g_ref[...]).astype(BF16)

    o_ref[...] = _dot(h_ref[...], w_ref[...]).astype(o_ref.dtype)


def norm_matmul(x, g, w, *, tm, tn, out_dtype):
    m, k = x.shape
    n = w.shape[1]
    return pl.pallas_call(
        _norm_matmul_kernel,
        grid=(m // tm, n // tn),
        in_specs=[pl.BlockSpec((tm, k), lambda i, j: (i, 0)),
                  pl.BlockSpec((1, k), lambda i, j: (0, 0)),
                  pl.BlockSpec((k, tn), lambda i, j: (0, j))],
        out_specs=pl.BlockSpec((tm, tn), lambda i, j: (i, j)),
        out_shape=jax.ShapeDtypeStruct((m, n), out_dtype),
        scratch_shapes=[pltpu.VMEM((tm, k), BF16)],
        compiler_params=_cparams(("parallel", "arbitrary")),
        name="norm_matmul",
    )(x, g.reshape(1, k), w)


def _out_proj_kernel(yt_ref, ym_ref, wt_ref, wm_ref, x_ref, g_ref, o_ref, h_ref, *, norm_tok):
    @pl.when(pl.program_id(1) == 0)
    def _():
        if norm_tok:
            y = yt_ref[...].astype(F32)
            ms = jnp.mean(y * y, axis=-1, keepdims=True)
            h_ref[...] = (y * lax.rsqrt(ms + EPS) * g_ref[...]).astype(BF16)
        else:
            h_ref[...] = yt_ref[...].astype(BF16)

    acc = _dot(h_ref[...], wt_ref[...]) + _dot(ym_ref[...], wm_ref[...])
    o_ref[...] = x_ref[...] + acc


def out_proj(y_tok, y_mem, w_tok, w_mem, x, gain, *, norm_tok, tm, tn):
    m, kt = y_tok.shape
    km = y_mem.shape[1]
    n = w_tok.shape[1]
    return pl.pallas_call(
        functools.partial(_out_proj_kernel, norm_tok=norm_tok),
        grid=(m // tm, n // tn),
        in_specs=[pl.BlockSpec((tm, kt), lambda i, j: (i, 0)),
                  pl.BlockSpec((tm, km), lambda i, j: (i, 0)),
                  pl.BlockSpec((kt, tn), lambda i, j: (0, j)),
                  pl.BlockSpec((km, tn), lambda i, j: (0, j)),
                  pl.BlockSpec((tm, tn), lambda i, j: (i, j)),
                  pl.BlockSpec((1, kt), lambda i, j: (0, 0))],
        out_specs=pl.BlockSpec((tm, tn), lambda i, j: (i, j)),
        out_shape=jax.ShapeDtypeStruct((m, n), F32),
        scratch_shapes=[pltpu.VMEM((tm, kt), BF16)],
        compiler_params=_cparams(("parallel", "arbitrary")),
        name="out_proj",
    )(y_tok, y_mem, w_tok, w_mem, x, gain.reshape(1, kt))


def _mem_attn_kernel(q_ref, gate_ref, kv_ref, qn_ref, kn_ref, o_ref):
    hd = MEM_HEAD_DIM
    scale = hd ** -0.5
    for h in range(MEM_HEADS):
        q = q_ref[:, h * hd:(h + 1) * hd].astype(F32)
        q = q * lax.rsqrt(jnp.mean(q * q, axis=-1, keepdims=True) + EPS) * qn_ref[...]
        k = kv_ref[:, h * hd:(h + 1) * hd].astype(F32)
        k = k * lax.rsqrt(jnp.mean(k * k, axis=-1, keepdims=True) + EPS) * kn_ref[...]
        v = kv_ref[:, W_MEM + h * hd:W_MEM + (h + 1) * hd].astype(BF16)
        s = _dot_nt((q * scale).astype(BF16), k.astype(BF16))
        e = jnp.exp(s - jnp.max(s, axis=-1, keepdims=True))
        o = _dot(e.astype(BF16), v) / jnp.sum(e, axis=-1, keepdims=True)
        gate = gate_ref[:, h * hd:(h + 1) * hd].astype(F32)
        o_ref[:, h * hd:(h + 1) * hd] = (o * _silu(gate)).astype(o_ref.dtype)


def mem_attn(proj, kv, qn, kn, *, q_off, seq, tq):
    m = proj.shape[0]
    nq = seq // tq
    qb = q_off // W_MEM
    return pl.pallas_call(
        _mem_attn_kernel,
        grid=(m // tq,),
        in_specs=[pl.BlockSpec((tq, W_MEM), lambda i: (i, qb)),
                  pl.BlockSpec((tq, W_MEM), lambda i: (i, qb + 1)),
                  pl.BlockSpec((MEM_TOKENS, 2 * W_MEM), lambda i: (i // nq, 0)),
                  pl.BlockSpec((1, MEM_HEAD_DIM), lambda i: (0, 0)),
                  pl.BlockSpec((1, MEM_HEAD_DIM), lambda i: (0, 0))],
        out_specs=pl.BlockSpec((tq, W_MEM), lambda i: (i, 0)),
        out_shape=jax.ShapeDtypeStruct((m, W_MEM), BF16),
        compiler_params=_cparams(("parallel",)),
        name="mem_attn",
    )(proj, proj, kv, qn.reshape(1, -1), kn.reshape(1, -1))


def _pool_kernel(v_ref, halo_ref, gate_ref, w_ref, scale_ref, o_ref, *, tm, seq):
    g = pl.program_id(0)
    i = pl.program_id(1)
    win = jnp.left_shift(2, g)
    t_seq = lax.rem(i * tm, seq)
    r = lax.broadcasted_iota(jnp.int32, (tm, 1), 0)
    c = lax.broadcasted_iota(jnp.int32, (1, tm), 1)
    a_main = jnp.where((c <= r) & (c > r - win), 1.0, 0.0).astype(BF16)
    ch = lax.broadcasted_iota(jnp.int32, (1, POOL_HALO), 1)
    a_halo = jnp.where((ch > r + POOL_HALO - win) & (t_seq > 0), 1.0, 0.0).astype(BF16)
    v = v_ref[...]
    wsum = _dot(a_main, v.astype(BF16)) + _dot(a_halo, halo_ref[...].astype(BF16))
    cnt = jnp.minimum(t_seq + r + 1, win).astype(F32)
    mix = wsum / cnt - v.astype(F32)
    out = _dot(mix.astype(BF16), w_ref[...]) * scale_ref[...]
    o_ref[...] = (out * _silu(gate_ref[...].astype(F32))).astype(o_ref.dtype)


def pool_mixer(proj, pool_w, pool_scale, *, seq, tm):
    m = proj.shape[0]
    hb = tm // POOL_HALO
    return pl.pallas_call(
        functools.partial(_pool_kernel, tm=tm, seq=seq),
        grid=(POOL_GROUPS, m // tm),
        in_specs=[pl.BlockSpec((tm, POOL_GW), lambda g, i: (i, g)),
                  pl.BlockSpec((POOL_HALO, POOL_GW), lambda g, i: (jnp.maximum(i * hb - 1, 0), g)),
                  pl.BlockSpec((tm, POOL_GW), lambda g, i: (i, POOL_GROUPS + g)),
                  pl.BlockSpec((None, POOL_GW, POOL_GW), lambda g, i: (g, 0, 0)),
                  pl.BlockSpec((1, POOL_GW), lambda g, i: (0, g))],
        out_specs=pl.BlockSpec((tm, POOL_GW), lambda g, i: (i, g)),
        out_shape=jax.ShapeDtypeStruct((m, W_TOK), BF16),
        compiler_params=_cparams(("parallel", "parallel")),
        name="pool_mixer",
    )(proj, proj, proj, pool_w.astype(BF16), pool_scale.reshape(1, W_TOK))


def _sgu_kernel(u_ref, v_ref, gate_ref, vn_ref, w_ref, bt_ref, o_ref, vs_ref, *, tm):
    ssq = jnp.zeros((tm, 1), F32)
    for g in range(SGU_GROUPS):
        sl = slice(g * SGU_GW, (g + 1) * SGU_GW)
        vg = _gelu_exact(v_ref[:, sl].astype(F32))
        vs_ref[:, sl] = vg
        ssq = ssq + jnp.sum(vg * vg, axis=-1, keepdims=True)
    inv = lax.rsqrt(ssq * (1.0 / W_TOK) + EPS)
    r = lax.broadcasted_iota(jnp.int32, (SGU_CHUNK, SGU_CHUNK), 0)
    c = lax.broadcasted_iota(jnp.int32, (SGU_CHUNK, SGU_CHUNK), 1)
    for g in range(SGU_GROUPS):
        sl = slice(g * SGU_GW, (g + 1) * SGU_GW)
        w = jnp.where(r >= c, w_ref[g], 0.0).astype(BF16)
        vn = (vs_ref[:, sl] * inv * vn_ref[:, sl]).astype(BF16)
        for ck in range(tm // SGU_CHUNK):
            rs = slice(ck * SGU_CHUNK, (ck + 1) * SGU_CHUNK)
            mixed = _dot(w, vn[rs]) + bt_ref[:, g:g + 1]
            u = _gelu_exact(u_ref[rs, sl].astype(F32))
            o_ref[rs, sl] = (u * mixed * _silu(gate_ref[rs, sl].astype(F32))).astype(o_ref.dtype)


def sgu_mixer(proj, v_norm, sgu_w, sgu_b, *, tm):
    m = proj.shape[0]
    return pl.pallas_call(
        functools.partial(_sgu_kernel, tm=tm),
        grid=(m // tm,),
        in_specs=[pl.BlockSpec((tm, W_TOK), lambda i: (i, 0)),
                  pl.BlockSpec((tm, W_TOK), lambda i: (i, 1)),
                  pl.BlockSpec((tm, W_TOK), lambda i: (i, 2)),
                  pl.BlockSpec((1, W_TOK), lambda i: (0, 0)),
                  pl.BlockSpec((SGU_GROUPS, SGU_CHUNK, SGU_CHUNK), lambda i: (0, 0, 0)),
                  pl.BlockSpec((SGU_CHUNK, SGU_GROUPS), lambda i: (0, 0))],
        out_specs=pl.BlockSpec((tm, W_TOK), lambda i: (i, 0)),
        out_shape=jax.ShapeDtypeStruct((m, W_TOK), BF16),
        scratch_shapes=[pltpu.VMEM((tm, W_TOK), F32)],
        compiler_params=_cparams(("parallel",)),
        name="sgu_mixer",
    )(proj, proj, proj, v_norm.reshape(1, W_TOK), sgu_w, sgu_b.T)


def _conv_silu(x, prev, w, b):
    acc = x * w[CONV_W - 1:CONV_W, :] + b
    row8 = lax.broadcasted_iota(jnp.int32, (8, 1), 0)
    for k in range(1, CONV_W):
        xs = pltpu.roll(x, k, 0)
        head = jnp.where(row8 < k, pltpu.roll(prev, k, 0), xs[0:8])
        xs = jnp.concatenate([head, xs[8:]], axis=0)
        acc = acc + xs * w[CONV_W - 1 - k:CONV_W - k, :]
    return _silu(acc)


def _expand_heads(cols, n_rows):
    lane_head = lax.broadcasted_iota(jnp.int32, (1, SSM_GX), 1) // SSM_HEAD_DIM
    out = jnp.zeros((n_rows, SSM_GX), F32)
    for r in range(SSM_RANK):
        out = jnp.where(lane_head == r, cols[:, r:r + 1], out)
    return out


def _ssd_kernel(x_ref, b_ref, c_ref, dt_ref, z_ref, cwx_ref, cwb_ref, cwc_ref, cbx_ref, cbb_ref, cbc_ref,
                dtb_ref, alog_ref, dexp_ref, o_ref, state_ref, px_ref, pb_ref, pc_ref):
    L = SSD_CHUNK
    ck = pl.program_id(2)

    @pl.when(ck == 0)
    def _():
        state_ref[...] = jnp.zeros_like(state_ref)
        px_ref[...] = jnp.zeros_like(px_ref)
        pb_ref[...] = jnp.zeros_like(pb_ref)
        pc_ref[...] = jnp.zeros_like(pc_ref)

    x_raw = x_ref[...].astype(F32)
    b_raw = b_ref[...].astype(F32)
    c_raw = c_ref[...].astype(F32)
    x = _conv_silu(x_raw, px_ref[...], cwx_ref[...], cbx_ref[...])
    bm = _conv_silu(b_raw, pb_ref[...], cwb_ref[...], cbb_ref[...])
    cm = _conv_silu(c_raw, pc_ref[...], cwc_ref[...], cbc_ref[...])
    px_ref[...] = x_raw[L - 8:L]
    pb_ref[...] = b_raw[L - 8:L]
    pc_ref[...] = c_raw[L - 8:L]

    dt = jax.nn.softplus(dt_ref[...].astype(F32) + dtb_ref[...])
    lane = lax.broadcasted_iota(jnp.int32, (1, LANES), 1)
    dt = jnp.where(lane < SSM_RANK, dt, 0.0)
    dta = dt * (-jnp.exp(alog_ref[...]))
    r = lax.broadcasted_iota(jnp.int32, (L, L), 0)
    c = lax.broadcasted_iota(jnp.int32, (L, L), 1)
    tri = r >= c
    tril = jnp.where(tri, 1.0, 0.0).astype(BF16)
    hi, mid, lo = _split3(dta)
    a_col = _dot(tril, hi) + _dot(tril, mid) + _dot(tril, lo)
    a_row = a_col.T
    dt_row = dt.T
    a_end = a_col[L - 1:L, :]

    xb = x.astype(BF16)
    cb = _dot_nt(cm.astype(BF16), bm.astype(BF16))
    lane_lo = lax.broadcasted_iota(jnp.int32, (1, LANES), 1) < SSM_HEAD_DIM
    y_parts = []
    for j in range(SSM_RANK // 2):
        xp = xb[:, j * LANES:(j + 1) * LANES]
        ys = []
        for r_ in (2 * j, 2 * j + 1):
            diff = a_col[:, r_:r_ + 1] - a_row[r_:r_ + 1, :]
            mm = cb * jnp.exp(jnp.where(tri, diff, NEG_INF)) * dt_row[r_:r_ + 1, :]
            ys.append(_dot(mm.astype(BF16), xp))
        y_parts.append(jnp.where(lane_lo, ys[0], ys[1]))
    y_in = jnp.concatenate(y_parts, axis=1)

    state = state_ref[...]
    y_st = _dot(cm.astype(BF16), state.astype(BF16)) * _expand_heads(jnp.exp(a_col), L)
    w_col = jnp.exp(a_end - a_col) * dt
    xw = (x * _expand_heads(w_col, L)).astype(BF16)
    state_ref[...] = state * _expand_heads(jnp.exp(a_end), 1) + _dot(bm.T.astype(BF16), xw)

    y = y_in + y_st + dexp_ref[...] * x
    o_ref[...] = (y * _silu(z_ref[...].astype(F32))).astype(o_ref.dtype)


def ssd_mixer(proj, conv_w, conv_b, dt_bias, a_log, d_skip, *, batch, seq, offs):
    m = proj.shape[0]
    L = SSD_CHUNK
    nc = seq // L
    G = SSM_GROUPS
    xo, bo, co, dto = (offs[k] for k in ("xs", "B", "C", "dt"))

    def pad_heads(p):
        return jnp.pad(p.reshape(G, 1, SSM_RANK), ((0, 0), (0, 0), (0, LANES - SSM_RANK)))

    d_exp = jnp.repeat(d_skip, SSM_HEAD_DIM).reshape(G, 1, SSM_GX)
    cb2 = conv_b.reshape(1, -1)
    row = lambda b, g, c: b * nc + c
    return pl.pallas_call(
        _ssd_kernel,
        grid=(batch, G, nc),
        in_specs=[pl.BlockSpec((L, SSM_GX), lambda b, g, c: (row(b, g, c), xo // SSM_GX + g)),
                  pl.BlockSpec((L, LANES), lambda b, g, c: (row(b, g, c), bo // LANES + g)),
                  pl.BlockSpec((L, LANES), lambda b, g, c: (row(b, g, c), co // LANES + g)),
                  pl.BlockSpec((L, LANES), lambda b, g, c: (row(b, g, c), dto // LANES + g)),
                  pl.BlockSpec((L, SSM_GX), lambda b, g, c: (row(b, g, c), g)),
                  pl.BlockSpec((CONV_W, SSM_GX), lambda b, g, c: (0, g)),
                  pl.BlockSpec((CONV_W, LANES), lambda b, g, c: (0, W_TOK // LANES + g)),
                  pl.BlockSpec((CONV_W, LANES), lambda b, g, c: (0, (W_TOK + G * SSM_STATE) // LANES + g)),
                  pl.BlockSpec((1, SSM_GX), lambda b, g, c: (0, g)),
                  pl.BlockSpec((1, LANES), lambda b, g, c: (0, W_TOK // LANES + g)),
                  pl.BlockSpec((1, LANES), lambda b, g, c: (0, (W_TOK + G * SSM_STATE) // LANES + g)),
                  pl.BlockSpec((None, 1, LANES), lambda b, g, c: (g, 0, 0)),
                  pl.BlockSpec((None, 1, LANES), lambda b, g, c: (g, 0, 0)),
                  pl.BlockSpec((None, 1, SSM_GX), lambda b, g, c: (g, 0, 0))],
        out_specs=pl.BlockSpec((L, SSM_GX), lambda b, g, c: (row(b, g, c), g)),
        out_shape=jax.ShapeDtypeStruct((m, W_TOK), BF16),
        scratch_shapes=[pltpu.VMEM((SSM_STATE, SSM_GX), F32),
                        pltpu.VMEM((8, SSM_GX), F32),
                        pltpu.VMEM((8, LANES), F32),
                        pltpu.VMEM((8, LANES), F32)],
        compiler_params=_cparams(("parallel", "parallel", "arbitrary")),
        name="ssd_mixer",
    )(proj, proj, proj, proj, proj, conv_w, conv_w, conv_w, cb2, cb2, cb2,
      pad_heads(dt_bias), pad_heads(a_log), d_exp)


def _rope_table_kernel(pos_ref, inv_ref, cos_ref, sin_ref):
    ang = pos_ref[...].astype(F32) * inv_ref[...]
    lane = lax.broadcasted_iota(jnp.int32, (1, NSA_HEAD_DIM), 1)
    cos_ref[...] = jnp.cos(ang)
    sin_ref[...] = jnp.where(lane < NSA_HEAD_DIM // 2, -1.0, 1.0) * jnp.sin(ang)


def rope_tables(positions, *, tm):
    m = positions.size
    inv = ROPE_THETA ** (-jnp.arange(0, NSA_HEAD_DIM, 2, dtype=F32) / NSA_HEAD_DIM)
    inv2 = jnp.concatenate([inv, inv]).reshape(1, NSA_HEAD_DIM)
    return pl.pallas_call(
        _rope_table_kernel,
        grid=(m // tm,),
        in_specs=[pl.BlockSpec((tm, 1), lambda i: (i, 0)),
                  pl.BlockSpec((1, NSA_HEAD_DIM), lambda i: (0, 0))],
        out_specs=[pl.BlockSpec((tm, NSA_HEAD_DIM), lambda i: (i, 0))] * 2,
        out_shape=[jax.ShapeDtypeStruct((m, NSA_HEAD_DIM), F32)] * 2,
        compiler_params=_cparams(("parallel",)),
        name="rope_tables",
    )(positions.reshape(m, 1), inv2)


def _nsa_prep_kernel(q_ref, kv_ref, cos_ref, sin_ref, qn_ref, kn_ref, qo_ref, cmp_ref, sel_ref, win_ref):
    d = NSA_HEAD_DIM
    cos = cos_ref[...]
    sin = sin_ref[...]

    def norm_rope(x, gain, scale):
        x = x.astype(F32)
        x = x * lax.rsqrt(jnp.mean(x * x, axis=-1, keepdims=True) + EPS) * gain
        return (x * cos + pltpu.roll(x, d // 2, 1) * sin) * scale

    for h in range(NSA_HEADS):
        sl = slice(h * d, (h + 1) * d)
        qo_ref[:, sl] = norm_rope(q_ref[:, sl], qn_ref[...], d ** -0.5).astype(qo_ref.dtype)
    for br, out in enumerate((cmp_ref, sel_ref, win_ref)):
        for h in range(NSA_KV_HEADS):
            ksl = slice((2 * br) * NSA_KV_W + h * d, (2 * br) * NSA_KV_W + (h + 1) * d)
            vsl = slice((2 * br + 1) * NSA_KV_W + h * d, (2 * br + 1) * NSA_KV_W + (h + 1) * d)
            out[:, h * d:(h + 1) * d] = norm_rope(kv_ref[:, ksl], kn_ref[...], 1.0).astype(out.dtype)
            out[:, NSA_KV_W + h * d:NSA_KV_W + (h + 1) * d] = kv_ref[:, vsl].astype(out.dtype)


def nsa_prep(proj, cos, sin, qn, kn, *, offs, tm):
    m = proj.shape[0]
    assert offs["q"] == 0 and offs["kv"] == W_TOK
    return pl.pallas_call(
        _nsa_prep_kernel,
        grid=(m // tm,),
        in_specs=[pl.BlockSpec((tm, W_TOK), lambda i: (i, 0)),
                  pl.BlockSpec((tm, W_TOK), lambda i: (i, 1)),
                  pl.BlockSpec((tm, NSA_HEAD_DIM), lambda i: (i, 0)),
                  pl.BlockSpec((tm, NSA_HEAD_DIM), lambda i: (i, 0)),
                  pl.BlockSpec((1, NSA_HEAD_DIM), lambda i: (0, 0)),
                  pl.BlockSpec((1, NSA_HEAD_DIM), lambda i: (0, 0))],
        out_specs=[pl.BlockSpec((tm, W_TOK), lambda i: (i, 0))] + [pl.BlockSpec((tm, 2 * NSA_KV_W), lambda i: (i, 0))] * 3,
        out_shape=[jax.ShapeDtypeStruct((m, W_TOK), BF16)] + [jax.ShapeDtypeStruct((m, 2 * NSA_KV_W), BF16)] * 3,
        compiler_params=_cparams(("parallel",)),
        name="nsa_prep",
    )(proj, proj, cos, sin, qn.reshape(1, -1), kn.reshape(1, -1))


def _nsa_compress_kernel(r_ref, pos_ref, kw1_ref, kw2_ref, vw1_ref, vw2_ref, kc_ref, vc_ref):
    d = NSA_HEAD_DIM
    half = CMP_STRIDE * d
    n = r_ref.shape[0]
    posb = jnp.broadcast_to(pos_ref[...], (8, CMP_BLOCK * d)).astype(BF16)
    for off, w1_ref, w2_ref, out in ((0, kw1_ref, kw2_ref, kc_ref), (NSA_KV_W, vw1_ref, vw2_ref, vc_ref)):
        pos_term = _dot(posb, w1_ref[...])[0:1]
        for h in range(NSA_KV_HEADS):
            cat = jnp.concatenate(
                [r_ref[:, tt * 2 * NSA_KV_W + off + h * d: tt * 2 * NSA_KV_W + off + (h + 1) * d]
                 for tt in range(CMP_STRIDE)], axis=1)
            ha = _dot(cat, w1_ref[0:half, :])
            hb = _dot(cat, w1_ref[half:2 * half, :])
            hsum = ha + pltpu.roll(hb, n - 1, 0) + pos_term
            out[h] = _dot(_silu(hsum).astype(BF16), w2_ref[...]).astype(out.dtype)


def nsa_compress(cmp_in, cmp_pos, kw1, kw2, vw1, vw2, *, batch, seq):
    n = seq // CMP_STRIDE
    wide = CMP_STRIDE * 2 * NSA_KV_W
    d = NSA_HEAD_DIM
    r = cmp_in.reshape(batch * n, wide)
    full = lambda shape: pl.BlockSpec(shape, lambda b: (0,) * len(shape))
    return pl.pallas_call(
        _nsa_compress_kernel,
        grid=(batch,),
        in_specs=[pl.BlockSpec((n, wide), lambda b: (b, 0)),
                  full((1, CMP_BLOCK * d)), full((CMP_BLOCK * d, d)), full((d, d)),
                  full((CMP_BLOCK * d, d)), full((d, d))],
        out_specs=[pl.BlockSpec((None, NSA_KV_HEADS, n, d), lambda b: (b, 0, 0, 0))] * 2,
        out_shape=[jax.ShapeDtypeStruct((batch, NSA_KV_HEADS, n, d), BF16)] * 2,
        compiler_params=_cparams(("parallel",)),
        name="nsa_compress",
    )(r, cmp_pos.reshape(1, CMP_BLOCK * d), kw1.astype(BF16), kw2.astype(BF16), vw1.astype(BF16), vw2.astype(BF16))


def _stack_heads(q_ref, qs_ref, tq):
    for g in range(NSA_GROUP):
        qs_ref[g * tq:(g + 1) * tq, :] = q_ref[:, g * NSA_HEAD_DIM:(g + 1) * NSA_HEAD_DIM]


def _store_gated(o3, gl_ref, o_ref, branch):
    for g in range(NSA_GROUP):
        lane = branch * NSA_GROUP + g
        gate = jax.nn.sigmoid(gl_ref[:, lane:lane + 1].astype(F32))
        o_ref[:, g * NSA_HEAD_DIM:(g + 1) * NSA_HEAD_DIM] = (o3[g] * gate).astype(o_ref.dtype)


def _nsa_cmp_kernel(q_ref, kc_ref, vc_ref, gl_ref, o_ref, sel_ref, qs_ref, *, tq, n_sel):
    i = pl.program_id(2)
    nc = kc_ref.shape[0]
    G = NSA_GROUP
    _stack_heads(q_ref, qs_ref, tq)
    s = _dot_nt(qs_ref[...], kc_ref[...]).reshape(G, tq, nc)
    t = i * tq + lax.broadcasted_iota(jnp.int32, (tq, 1), 0)
    cend = lax.broadcasted_iota(jnp.int32, (1, nc), 1) * CMP_STRIDE + (CMP_BLOCK - 1)
    cmask = cend <= t
    s = jnp.where(cmask[None], s, NEG_INF)
    e = jnp.exp(s - jnp.max(s, axis=-1, keepdims=True))
    any_vis = jnp.where(t >= CMP_BLOCK - 1, 1.0, 0.0)
    p = e * (any_vis / jnp.sum(e, axis=-1, keepdims=True))
    o3 = _dot(p.reshape(G * tq, nc).astype(BF16), vc_ref[...]).reshape(G, tq, NSA_HEAD_DIM)
    _store_gated(o3, gl_ref, o_ref, 0)

    psum = jnp.sum(p, axis=0)
    cs = lax.broadcasted_iota(jnp.int32, (nc, LANES), 0) * CMP_STRIDE
    js = lax.broadcasted_iota(jnp.int32, (nc, LANES), 1) * SEL_BLOCK
    agg = jnp.where((cs < js + SEL_BLOCK) & (cs + CMP_BLOCK - 1 >= js), 1.0, 0.0).astype(BF16)
    hi, mid, lo = _split3(psum)
    imp = _dot(hi, agg) + _dot(mid, agg) + _dot(lo, agg)
    j = lax.broadcasted_iota(jnp.int32, (1, LANES), 1)
    bt = t // SEL_BLOCK
    forced = (j == 0) | (j == bt) | (j == bt - 1)
    score = jnp.where(forced, SEL_BIG, jnp.where(j <= bt, imp, -SEL_BIG))
    score = jnp.where(j < n_sel, score, -3.0 * SEL_BIG)
    sc = score.T
    rowi = lax.broadcasted_iota(jnp.int32, (LANES, tq), 0).astype(F32)
    sel = jnp.zeros((LANES, tq), F32)
    for _ in range(min(SEL_TOPK, n_sel)):
        mx = jnp.max(sc, axis=0, keepdims=True)
        idx = jnp.min(jnp.where(sc == mx, rowi, float(LANES)), axis=0, keepdims=True)
        pick = rowi == idx
        sel = jnp.where(pick, 1.0, sel)
        sc = jnp.where(pick, -jnp.inf, sc)
    sel_ref[...] = sel.T.astype(sel_ref.dtype)


def nsa_cmp_attn(q_r, kc, vc, proj, *, batch, seq, gl_off, tq):
    m = q_r.shape[0]
    nq = seq // tq
    n_cmp = kc.shape[2]
    gw = NSA_GROUP * NSA_HEAD_DIM
    glb = gl_off // LANES
    return pl.pallas_call(
        functools.partial(_nsa_cmp_kernel, tq=tq, n_sel=seq // SEL_BLOCK),
        grid=(batch, NSA_KV_HEADS, nq),
        in_specs=[pl.BlockSpec((tq, gw), lambda b, h, i: (b * nq + i, h)),
                  pl.BlockSpec((None, None, n_cmp, NSA_HEAD_DIM), lambda b, h, i: (b, h, 0, 0)),
                  pl.BlockSpec((None, None, n_cmp, NSA_HEAD_DIM), lambda b, h, i: (b, h, 0, 0)),
                  pl.BlockSpec((tq, LANES), lambda b, h, i: (b * nq + i, glb + h))],
        out_specs=[pl.BlockSpec((tq, gw), lambda b, h, i: (b * nq + i, h)),
                   pl.BlockSpec((None, None, tq, LANES), lambda b, h, i: (b, h, i, 0))],
        out_shape=[jax.ShapeDtypeStruct((m, W_TOK), BF16),
                   jax.ShapeDtypeStruct((batch, NSA_KV_HEADS, seq, LANES), BF16)],
        scratch_shapes=[pltpu.VMEM((NSA_GROUP * tq, NSA_HEAD_DIM), BF16)],
        compiler_params=_cparams(("parallel", "parallel", "parallel")),
        name="nsa_cmp_attn",
    )(q_r, kc, vc, proj)


def _nsa_flash_kernel(qi_ref, ki_ref, first_ref, last_ref, q_ref, k_ref, v_ref, sel_ref, gl_ref, o_ref,
                      qs_ref, m_ref, l_ref, acc_ref, *, tq, tk, branch):
    step = pl.program_id(2)
    qi = qi_ref[step]
    ki = ki_ref[step]
    G = NSA_GROUP

    @pl.when(first_ref[step] == 1)
    def _():
        _stack_heads(q_ref, qs_ref, tq)
        m_ref[...] = jnp.full_like(m_ref, NEG_INF)
        l_ref[...] = jnp.zeros_like(l_ref)
        acc_ref[...] = jnp.zeros_like(acc_ref)

    s = _dot_nt(qs_ref[...], k_ref[...]).reshape(G, tq, tk)
    t = qi * tq + lax.broadcasted_iota(jnp.int32, (tq, 1), 0)
    kp = ki * tk + lax.broadcasted_iota(jnp.int32, (1, tk), 1)
    if branch == 1:
        jrow = lax.broadcasted_iota(jnp.int32, (LANES, 1), 0)
        expand = jnp.where(jrow == kp // SEL_BLOCK, 1.0, 0.0).astype(BF16)
        mask = (_dot(sel_ref[...], expand) > 0.5) & (kp <= t)
    else:
        mask = (kp <= t) & (kp > t - WINDOW)
    s = jnp.where(mask[None], s, NEG_INF)
    m_old = m_ref[...]
    m_new = jnp.maximum(m_old, jnp.max(s, axis=-1, keepdims=True))
    alpha = jnp.exp(m_old - m_new)
    p = jnp.where(mask[None], jnp.exp(s - m_new), 0.0)
    l_ref[...] = alpha * l_ref[...] + jnp.sum(p, axis=-1, keepdims=True)
    pv = _dot(p.reshape(G * tq, tk).astype(BF16), v_ref[...]).reshape(G, tq, NSA_HEAD_DIM)
    acc_ref[...] = alpha * acc_ref[...] + pv
    m_ref[...] = m_new

    @pl.when(last_ref[step] == 1)
    def _():
        _store_gated(acc_ref[...] / l_ref[...], gl_ref, o_ref, branch)


def nsa_flash_attn(q_r, kv, sel, proj, *, batch, seq, gl_off, branch, tq, tk):
    m = q_r.shape[0]
    nq = seq // tq
    nk = seq // tk
    gw = NSA_GROUP * NSA_HEAD_DIM
    glb = gl_off // LANES
    steps = []
    for i in range(nq):
        hi_k = ((i + 1) * tq - 1) // tk
        lo_k = 0 if branch == 1 else max(0, (i * tq - WINDOW + 1) // tk)
        for kk in range(lo_k, hi_k + 1):
            steps.append((i, kk, int(kk == lo_k), int(kk == hi_k)))
    tabs = [jnp.asarray(np.array([s_[c] for s_ in steps], np.int32)) for c in range(4)]
    grid_spec = pltpu.PrefetchScalarGridSpec(
        num_scalar_prefetch=4,
        grid=(batch, NSA_KV_HEADS, len(steps)),
        in_specs=[pl.BlockSpec((tq, gw), lambda b, h, s, qi, ki, fi, la: (b * nq + qi[s], h)),
                  pl.BlockSpec((tk, NSA_HEAD_DIM), lambda b, h, s, qi, ki, fi, la: (b * nk + ki[s], h)),
                  pl.BlockSpec((tk, NSA_HEAD_DIM), lambda b, h, s, qi, ki, fi, la: (b * nk + ki[s], NSA_KV_HEADS + h)),
                  pl.BlockSpec((None, None, tq, LANES), lambda b, h, s, qi, ki, fi, la: (b, h, qi[s], 0)),
                  pl.BlockSpec((tq, LANES), lambda b, h, s, qi, ki, fi, la: (b * nq + qi[s], glb + h))],
        out_specs=pl.BlockSpec((tq, gw), lambda b, h, s, qi, ki, fi, la: (b * nq + qi[s], h)),
        scratch_shapes=[pltpu.VMEM((NSA_GROUP * tq, NSA_HEAD_DIM), BF16),
                        pltpu.VMEM((NSA_GROUP, tq, 1), F32),
                        pltpu.VMEM((NSA_GROUP, tq, 1), F32),
                        pltpu.VMEM((NSA_GROUP, tq, NSA_HEAD_DIM), F32)])
    return pl.pallas_call(
        functools.partial(_nsa_flash_kernel, tq=tq, tk=tk, branch=branch),
        grid_spec=grid_spec,
        out_shape=jax.ShapeDtypeStruct((m, W_TOK), BF16),
        compiler_params=_cparams(("parallel", "parallel", "arbitrary")),
        name="nsa_sel_attn" if branch == 1 else "nsa_win_attn",
    )(*tabs, q_r, kv, kv, sel, proj)


def _nsa_combine_kernel(a_ref, b_ref, c_ref, gate_ref, o_ref):
    o = a_ref[...].astype(F32) + b_ref[...].astype(F32) + c_ref[...].astype(F32)
    o_ref[...] = (o * _silu(gate_ref[...].astype(F32))).astype(o_ref.dtype)


def nsa_combine(o_cmp, o_sel, o_win, proj, *, gate_off, tm):
    m = o_cmp.shape[0]
    gb = gate_off // W_TOK
    assert gate_off % W_TOK == 0
    blk = lambda cb: pl.BlockSpec((tm, W_TOK), lambda i: (i, cb))
    return pl.pallas_call(
        _nsa_combine_kernel,
        grid=(m // tm,),
        in_specs=[blk(0), blk(0), blk(0), blk(gb)],
        out_specs=blk(0),
        out_shape=jax.ShapeDtypeStruct((m, W_TOK), BF16),
        compiler_params=_cparams(("parallel",)),
        name="nsa_combine",
    )(o_cmp, o_sel, o_win, proj)


def _nsa_layout(w_in):
    kv_end = W_TOK + 6 * NSA_KV_W
    gl_end = kv_end + 3 * NSA_HEADS
    k = w_in.shape[0]
    glw = w_in[:, kv_end:gl_end].reshape(k, NSA_KV_HEADS, NSA_GROUP, 3)
    glw = glw.transpose(0, 1, 3, 2).reshape(k, NSA_KV_HEADS, 3 * NSA_GROUP)
    glw = jnp.pad(glw, ((0, 0), (0, 0), (0, LANES - 3 * NSA_GROUP))).reshape(k, NSA_KV_HEADS * LANES)
    pad_w = 2 * W_TOK - kv_end - NSA_KV_HEADS * LANES
    w = jnp.concatenate([w_in[:, :kv_end], glw, jnp.zeros((k, pad_w), w_in.dtype), w_in[:, gl_end:]], axis=1)
    offs = {"q": 0, "kv": W_TOK, "gl": kv_end, "gate": 2 * W_TOK, "memq": 3 * W_TOK}
    return w.astype(BF16), offs


def _ssd_layout(w_in):
    k = w_in.shape[0]
    xbc_end = W_TOK + W_TOK + 2 * SSM_GROUPS * SSM_STATE
    dt_end = xbc_end + SSM_HEADS
    dtw = w_in[:, xbc_end:dt_end].reshape(k, SSM_GROUPS, SSM_RANK)
    dtw = jnp.pad(dtw, ((0, 0), (0, 0), (0, LANES - SSM_RANK))).reshape(k, SSM_GROUPS * LANES)
    w = jnp.concatenate([w_in[:, :xbc_end], w_in[:, dt_end:], dtw], axis=1)
    offs = {"z": 0, "xs": W_TOK, "B": 2 * W_TOK, "C": 2 * W_TOK + SSM_GROUPS * SSM_STATE,
            "memq": xbc_end, "dt": xbc_end + 2 * W_MEM}
    return w.astype(BF16), offs


def nsa_mixer(proj, positions, qn, kn, cmp_pos, kw1, kw2, vw1, vw2, *, batch, seq, offs):
    cos, sin = rope_tables(positions, tm=min(1024, batch * seq))
    q_r, cmp_in, sel_kv, win_kv = nsa_prep(proj, cos, sin, qn, kn, offs=offs, tm=256)
    kc, vc = nsa_compress(cmp_in, cmp_pos, kw1, kw2, vw1, vw2, batch=batch, seq=seq)
    o_cmp, sel = nsa_cmp_attn(q_r, kc, vc, proj, batch=batch, seq=seq, gl_off=offs["gl"], tq=256)
    o_sel = nsa_flash_attn(q_r, sel_kv, sel, proj, batch=batch, seq=seq, gl_off=offs["gl"], branch=1, tq=256, tk=256)
    o_win = nsa_flash_attn(q_r, win_kv, sel, proj, batch=batch, seq=seq, gl_off=offs["gl"], branch=2, tq=256, tk=256)
    return nsa_combine(o_cmp, o_sel, o_win, proj, gate_off=offs["gate"], tm=256)


def _layer(x2, mem2, mem_norm, norm, w_in_b, w_out, mem_wkv, mem_qn, mem_kn, mixer, memq_off, *, seq, gnorm=None):
    proj = norm_matmul(x2, norm, w_in_b, tm=1024 if x2.shape[0] % 1024 == 0 else x2.shape[0], tn=512, out_dtype=BF16)
    kv = norm_matmul(mem2, mem_norm, mem_wkv.astype(BF16), tm=mem2.shape[0] if mem2.shape[0] < 512 else 512,
                     tn=512, out_dtype=F32)
    y_mem = mem_attn(proj, kv, mem_qn, mem_kn, q_off=memq_off, seq=seq, tq=min(512, seq))
    y_tok = mixer(proj)
    w_tok = w_out[:W_TOK].astype(BF16)
    w_mem = w_out[W_TOK:].astype(BF16)
    gain = gnorm if gnorm is not None else jnp.ones((W_TOK,), F32)
    return out_proj(y_tok, y_mem, w_tok, w_mem, x2, gain, norm_tok=gnorm is not None,
                    tm=512 if x2.shape[0] % 512 == 0 else x2.shape[0], tn=512)


def kernel(x, mem, positions, mem_norm, l0_norm, l0_w_in, l0_w_out, l0_mem_wkv, l0_mem_qnorm, l0_mem_knorm, l0_pool_w, l0_pool_scale, l1_norm, l1_w_in, l1_w_out, l1_mem_wkv, l1_mem_qnorm, l1_mem_knorm, l1_qnorm, l1_knorm, l1_cmp_pos, l1_cmp_k_w1, l1_cmp_k_w2, l1_cmp_v_w1, l1_cmp_v_w2, l2_norm, l2_w_in, l2_w_out, l2_mem_wkv, l2_mem_qnorm, l2_mem_knorm, l2_conv_w, l2_conv_b, l2_dt_bias, l2_A_log, l2_D, l2_gnorm, l3_norm, l3_w_in, l3_w_out, l3_mem_wkv, l3_mem_qnorm, l3_mem_knorm, l3_v_norm, l3_sgu_w, l3_sgu_b):
    batch, seq, d = x.shape
    x2 = x.reshape(batch * seq, d)
    mem2 = mem.reshape(batch * MEM_TOKENS, d)

    x2 = _layer(x2, mem2, mem_norm, l0_norm, l0_w_in.astype(BF16), l0_w_out, l0_mem_wkv, l0_mem_qnorm, l0_mem_knorm,
                lambda p: pool_mixer(p, l0_pool_w, l0_pool_scale, seq=seq, tm=256), 2 * W_TOK, seq=seq)

    w1, offs1 = _nsa_layout(l1_w_in)
    x2 = _layer(x2, mem2, mem_norm, l1_norm, w1, l1_w_out, l1_mem_wkv, l1_mem_qnorm, l1_mem_knorm,
                lambda p: nsa_mixer(p, positions, l1_qnorm, l1_knorm, l1_cmp_pos, l1_cmp_k_w1, l1_cmp_k_w2,
                                    l1_cmp_v_w1, l1_cmp_v_w2, batch=batch, seq=seq, offs=offs1),
                offs1["memq"], seq=seq)

    w2, offs2 = _ssd_layout(l2_w_in)
    x2 = _layer(x2, mem2, mem_norm, l2_norm, w2, l2_w_out, l2_mem_wkv, l2_mem_qnorm, l2_mem_knorm,
                lambda p: ssd_mixer(p, l2_conv_w, l2_conv_b, l2_dt_bias, l2_A_log, l2_D, batch=batch, seq=seq, offs=offs2),
                offs2["memq"], seq=seq, gnorm=l2_gnorm)

    x2 = _layer(x2, mem2, mem_norm, l3_norm, l3_w_in.astype(BF16), l3_w_out, l3_mem_wkv, l3_mem_qnorm, l3_mem_knorm,
                lambda p: sgu_mixer(p, l3_v_norm, l3_sgu_w, l3_sgu_b, tm=256), 3 * W_TOK, seq=seq)
    return x2.reshape(batch, seq, d)
```

```python
import functools
import math

import numpy as np
import jax
import jax.numpy as jnp
from jax import lax
from jax.experimental import pallas as pl
from jax.experimental.pallas import tpu as pltpu

F32 = jnp.float32
BF16 = jnp.bfloat16

D_MODEL = 2048
EPS = 1e-6
ROPE_THETA = 10000.0
NEG_INF = -1e30
SEL_BIG = 1e9
LOG2E = 1.0 / math.log(2.0)

W_TOK = 2 * D_MODEL
MEM_TOKENS = 256
MEM_HEADS = 4
MEM_HEAD_DIM = D_MODEL // 8
W_MEM = MEM_HEADS * MEM_HEAD_DIM

POOL_GROUPS = 4
POOL_GW = W_TOK // POOL_GROUPS
POOL_HALO = 16

NSA_HEADS = 32
NSA_KV_HEADS = 4
NSA_GROUP = NSA_HEADS // NSA_KV_HEADS
NSA_HEAD_DIM = 128
NSA_KV_W = NSA_KV_HEADS * NSA_HEAD_DIM
CMP_BLOCK = 32
CMP_STRIDE = 16
SEL_BLOCK = 64
SEL_TOPK = 16
WINDOW = 512

SSM_HEAD_DIM = 64
SSM_HEADS = W_TOK // SSM_HEAD_DIM
SSM_GROUPS = 8
SSM_RANK = SSM_HEADS // SSM_GROUPS
SSM_STATE = 128
CONV_W = 4
SSD_CHUNK = 256
SSM_GX = W_TOK // SSM_GROUPS

SGU_CHUNK = 128
SGU_GROUPS = 8
SGU_GW = W_TOK // SGU_GROUPS

LANES = 128
VMEM_LIMIT = 56 * 1024 * 1024


def _cparams(sem):
    return pltpu.CompilerParams(dimension_semantics=sem, vmem_limit_bytes=VMEM_LIMIT)


def _silu(x):
    return x * jax.nn.sigmoid(x)


def _gelu_exact(x):
    return 0.5 * x * (1.0 + lax.erf(x * (1.0 / math.sqrt(2.0))))


def _dot(a, b):
    return jnp.dot(a, b, preferred_element_type=F32)


def _dot_nt(a, b):
    return lax.dot_general(a, b, (((1,), (1,)), ((), ())), preferred_element_type=F32)


def _split3(x):
    hi = x.astype(BF16)
    r1 = x - hi.astype(F32)
    mid = r1.astype(BF16)
    lo = (r1 - mid.astype(F32)).astype(BF16)
    return hi, mid, lo


def _norm_matmul_kernel(x_ref, g_ref, w_ref, o_ref, h_ref):
    @pl.when(pl.program_id(1) == 0)
    def _():
        x = x_ref[...].astype(F32)
        ms = jnp.mean(x * x, axis=-1, keepdims=True)
        h_ref[...] = (x * lax.rsqrt(ms + EPS) * g_ref[...]).astype(BF16)

    o_ref[...] = _dot(h_ref[...], w_ref[...]).astype(o_ref.dtype)


def norm_matmul(x, g, w, *, tm, tn, out_dtype):
    m, k = x.shape
    n = w.shape[1]
    return pl.pallas_call(
        _norm_matmul_kernel,
        grid=(m // tm, n // tn),
        in_specs=[pl.BlockSpec((tm, k), lambda i, j: (i, 0)),
                  pl.BlockSpec((1, k), lambda i, j: (0, 0)),
                  pl.BlockSpec((k, tn), lambda i, j: (0, j))],
        out_specs=pl.BlockSpec((tm, tn), lambda i, j: (i, j)),
        out_shape=jax.ShapeDtypeStruct((m, n), out_dtype),
        scratch_shapes=[pltpu.VMEM((tm, k), BF16)],
        compiler_params=_cparams(("parallel", "arbitrary")),
        name="norm_matmul",
    )(x, g.reshape(1, k), w)


def _out_proj_kernel(yt_ref, ym_ref, wt_ref, wm_ref, x_ref, g_ref, o_ref, h_ref, *, norm_tok):
    @pl.when(pl.program_id(1) == 0)
    def _():
        if norm_tok:
            y = yt_ref[...].astype(F32)
            ms = jnp.mean(y * y, axis=-1, keepdims=True)
            h_ref[...] = (y * lax.rsqrt(ms + EPS) * g_ref[...]).astype(BF16)
        else:
            h_ref[...] = yt_ref[...].astype(BF16)

    acc = _dot(h_ref[...], wt_ref[...]) + _dot(ym_ref[...], wm_ref[...])
    o_ref[...] = x_ref[...] + acc


def out_proj(y_tok, y_mem, w_tok, w_mem, x, gain, *, norm_tok, tm, tn):
    m, kt = y_tok.shape
    km = y_mem.shape[1]
    n = w_tok.shape[1]
    return pl.pallas_call(
        functools.partial(_out_proj_kernel, norm_tok=norm_tok),
        grid=(m // tm, n // tn),
        in_specs=[pl.BlockSpec((tm, kt), lambda i, j: (i, 0)),
                  pl.BlockSpec((tm, km), lambda i, j: (i, 0)),
                  pl.BlockSpec((kt, tn), lambda i, j: (0, j)),
                  pl.BlockSpec((km, tn), lambda i, j: (0, j)),
                  pl.BlockSpec((tm, tn), lambda i, j: (i, j)),
                  pl.BlockSpec((1, kt), lambda i, j: (0, 0))],
        out_specs=pl.BlockSpec((tm, tn), lambda i, j: (i, j)),
        out_shape=jax.ShapeDtypeStruct((m, n), F32),
        scratch_shapes=[pltpu.VMEM((tm, kt), BF16)],
        compiler_params=_cparams(("parallel", "arbitrary")),
        name="out_proj",
    )(y_tok, y_mem, w_tok, w_mem, x, gain.reshape(1, kt))


def _mem_attn_kernel(q_ref, gate_ref, kv_ref, qn_ref, kn_ref, o_ref):
    hd = MEM_HEAD_DIM
    scale = hd ** -0.5
    for h in range(MEM_HEADS):
        q = q_ref[:, h * hd:(h + 1) * hd].astype(F32)
        q = q * lax.rsqrt(jnp.mean(q * q, axis=-1, keepdims=True) + EPS) * qn_ref[...]
        k = kv_ref[:, h * hd:(h + 1) * hd].astype(F32)
        k = k * lax.rsqrt(jnp.mean(k * k, axis=-1, keepdims=True) + EPS) * kn_ref[...]
        v = kv_ref[:, W_MEM + h * hd:W_MEM + (h + 1) * hd].astype(BF16)
        s = _dot_nt((q * scale).astype(BF16), k.astype(BF16))
        e = jnp.exp(s - jnp.max(s, axis=-1, keepdims=True))
        o = _dot(e.astype(BF16), v) / jnp.sum(e, axis=-1, keepdims=True)
        gate = gate_ref[:, h * hd:(h + 1) * hd].astype(F32)
        o_ref[:, h * hd:(h + 1) * hd] = (o * _silu(gate)).astype(o_ref.dtype)


def mem_attn(proj, kv, qn, kn, *, q_off, seq, tq):
    m = proj.shape[0]
    nq = seq // tq
    qb = q_off // W_MEM
    return pl.pallas_call(
        _mem_attn_kernel,
        grid=(m // tq,),
        in_specs=[pl.BlockSpec((tq, W_MEM), lambda i: (i, qb)),
                  pl.BlockSpec((tq, W_MEM), lambda i: (i, qb + 1)),
                  pl.BlockSpec((MEM_TOKENS, 2 * W_MEM), lambda i: (i // nq, 0)),
                  pl.BlockSpec((1, MEM_HEAD_DIM), lambda i: (0, 0)),
                  pl.BlockSpec((1, MEM_HEAD_DIM), lambda i: (0, 0))],
        out_specs=pl.BlockSpec((tq, W_MEM), lambda i: (i, 0)),
        out_shape=jax.ShapeDtypeStruct((m, W_MEM), BF16),
        compiler_params=_cparams(("parallel",)),
        name="mem_attn",
    )(proj, proj, kv, qn.reshape(1, -1), kn.reshape(1, -1))


def _pool_kernel(v_ref, halo_ref, gate_ref, w_ref, scale_ref, o_ref, *, tm, seq):
    g = pl.program_id(0)
    i = pl.program_id(1)
    win = jnp.left_shift(2, g)
    t_seq = lax.rem(i * tm, seq)
    r = lax.broadcasted_iota(jnp.int32, (tm, 1), 0)
    c = lax.broadcasted_iota(jnp.int32, (1, tm), 1)
    a_main = jnp.where((c <= r) & (c > r - win), 1.0, 0.0).astype(BF16)
    ch = lax.broadcasted_iota(jnp.int32, (1, POOL_HALO), 1)
    a_halo = jnp.where((ch > r + POOL_HALO - win) & (t_seq > 0), 1.0, 0.0).astype(BF16)
    v = v_ref[...]
    wsum = _dot(a_main, v.astype(BF16)) + _dot(a_halo, halo_ref[...].astype(BF16))
    cnt = jnp.minimum(t_seq + r + 1, win).astype(F32)
    mix = wsum / cnt - v.astype(F32)
    out = _dot(mix.astype(BF16), w_ref[...]) * scale_ref[...]
    o_ref[...] = (out * _silu(gate_ref[...].astype(F32))).astype(o_ref.dtype)


def pool_mixer(proj, pool_w, pool_scale, *, seq, tm):
    m = proj.shape[0]
    hb = tm // POOL_HALO
    return pl.pallas_call(
        functools.partial(_pool_kernel, tm=tm, seq=seq),
        grid=(POOL_GROUPS, m // tm),
        in_specs=[pl.BlockSpec((tm, POOL_GW), lambda g, i: (i, g)),
                  pl.BlockSpec((POOL_HALO, POOL_GW), lambda g, i: (jnp.maximum(i * hb - 1, 0), g)),
                  pl.BlockSpec((tm, POOL_GW), lambda g, i: (i, POOL_GROUPS + g)),
                  pl.BlockSpec((None, POOL_GW, POOL_GW), lambda g, i: (g, 0, 0)),
                  pl.BlockSpec((1, POOL_GW), lambda g, i: (0, g))],
        out_specs=pl.BlockSpec((tm, POOL_GW), lambda g, i: (i, g)),
        out_shape=jax.ShapeDtypeStruct((m, W_TOK), BF16),
        compiler_params=_cparams(("parallel", "parallel")),
        name="pool_mixer",
    )(proj, proj, proj, pool_w.astype(BF16), pool_scale.reshape(1, W_TOK))


def _sgu_kernel(u_ref, v_ref, gate_ref, vn_ref, w_ref, bt_ref, o_ref, vs_ref, *, tm):
    ssq = jnp.zeros((tm, 1), F32)
    for g in range(SGU_GROUPS):
        sl = slice(g * SGU_GW, (g + 1) * SGU_GW)
        vg = _gelu_exact(v_ref[:, sl].astype(F32))
        vs_ref[:, sl] = vg
        ssq = ssq + jnp.sum(vg * vg, axis=-1, keepdims=True)
    inv = lax.rsqrt(ssq * (1.0 / W_TOK) + EPS)
    r = lax.broadcasted_iota(jnp.int32, (SGU_CHUNK, SGU_CHUNK), 0)
    c = lax.broadcasted_iota(jnp.int32, (SGU_CHUNK, SGU_CHUNK), 1)
    for g in range(SGU_GROUPS):
        sl = slice(g * SGU_GW, (g + 1) * SGU_GW)
        w = jnp.where(r >= c, w_ref[g], 0.0).astype(BF16)
        vn = (vs_ref[:, sl] * inv * vn_ref[:, sl]).astype(BF16)
        for ck in range(tm // SGU_CHUNK):
            rs = slice(ck * SGU_CHUNK, (ck + 1) * SGU_CHUNK)
            mixed = _dot(w, vn[rs]) + bt_ref[:, g:g + 1]
            u = _gelu_exact(u_ref[rs, sl].astype(F32))
            o_ref[rs, sl] = (u * mixed * _silu(gate_ref[rs, sl].astype(F32))).astype(o_ref.dtype)


def sgu_mixer(proj, v_norm, sgu_w, sgu_b, *, tm):
    m = proj.shape[0]
    return pl.pallas_call(
        functools.partial(_sgu_kernel, tm=tm),
        grid=(m // tm,),
        in_specs=[pl.BlockSpec((tm, W_TOK), lambda i: (i, 0)),
                  pl.BlockSpec((tm, W_TOK), lambda i: (i, 1)),
                  pl.BlockSpec((tm, W_TOK), lambda i: (i, 2)),
                  pl.BlockSpec((1, W_TOK), lambda i: (0, 0)),
                  pl.BlockSpec((SGU_GROUPS, SGU_CHUNK, SGU_CHUNK), lambda i: (0, 0, 0)),
                  pl.BlockSpec((SGU_CHUNK, SGU_GROUPS), lambda i: (0, 0))],
        out_specs=pl.BlockSpec((tm, W_TOK), lambda i: (i, 0)),
        out_shape=jax.ShapeDtypeStruct((m, W_TOK), BF16),
        scratch_shapes=[pltpu.VMEM((tm, W_TOK), F32)],
        compiler_params=_cparams(("parallel",)),
        name="sgu_mixer",
    )(proj, proj, proj, v_norm.reshape(1, W_TOK), sgu_w, sgu_b.T)


def _conv_silu(x, prev, w, b):
    acc = x * w[CONV_W - 1:CONV_W, :] + b
    row8 = lax.broadcasted_iota(jnp.int32, (8, 1), 0)
    for k in range(1, CONV_W):
        xs = pltpu.roll(x, k, 0)
        head = jnp.where(row8 < k, pltpu.roll(prev, k, 0), xs[0:8])
        xs = jnp.concatenate([head, xs[8:]], axis=0)
        acc = acc + xs * w[CONV_W - 1 - k:CONV_W - k, :]
    return _silu(acc)


def _expand_heads(cols, n_rows):
    lane_head = lax.broadcasted_iota(jnp.int32, (1, SSM_GX), 1) // SSM_HEAD_DIM
    out = jnp.zeros((n_rows, SSM_GX), F32)
    for r in range(SSM_RANK):
        out = jnp.where(lane_head == r, cols[:, r:r + 1], out)
    return out


def _ssd_kernel(x_ref, b_ref, c_ref, dt_ref, z_ref, cwx_ref, cwb_ref, cwc_ref, cbx_ref, cbb_ref, cbc_ref,
                dtb_ref, alog_ref, dexp_ref, o_ref, state_ref, px_ref, pb_ref, pc_ref):
    L = SSD_CHUNK
    ck = pl.program_id(2)

    @pl.when(ck == 0)
    def _():
        state_ref[...] = jnp.zeros_like(state_ref)
        px_ref[...] = jnp.zeros_like(px_ref)
        pb_ref[...] = jnp.zeros_like(pb_ref)
        pc_ref[...] = jnp.zeros_like(pc_ref)

    x_raw = x_ref[...].astype(F32)
    b_raw = b_ref[...].astype(F32)
    c_raw = c_ref[...].astype(F32)
    x = _conv_silu(x_raw, px_ref[...], cwx_ref[...], cbx_ref[...])
    bm = _conv_silu(b_raw, pb_ref[...], cwb_ref[...], cbb_ref[...])
    cm = _conv_silu(c_raw, pc_ref[...], cwc_ref[...], cbc_ref[...])
    px_ref[...] = x_raw[L - 8:L]
    pb_ref[...] = b_raw[L - 8:L]
    pc_ref[...] = c_raw[L - 8:L]

    dt = jax.nn.softplus(dt_ref[...].astype(F32) + dtb_ref[...])
    lane = lax.broadcasted_iota(jnp.int32, (1, LANES), 1)
    dt = jnp.where(lane < SSM_RANK, dt, 0.0)
    dta = dt * (-jnp.exp(alog_ref[...]))
    r = lax.broadcasted_iota(jnp.int32, (L, L), 0)
    c = lax.broadcasted_iota(jnp.int32, (L, L), 1)
    tri = r >= c
    tril = jnp.where(tri, 1.0, 0.0).astype(BF16)
    hi, mid, lo = _split3(dta)
    a_col = _dot(tril, hi) + _dot(tril, mid) + _dot(tril, lo)
    a_row = a_col.T
    dt_row = dt.T
    a_end = a_col[L - 1:L, :]

    xb = x.astype(BF16)
    cb = _dot_nt(cm.astype(BF16), bm.astype(BF16))
    lane_lo = lax.broadcasted_iota(jnp.int32, (1, LANES), 1) < SSM_HEAD_DIM
    y_parts = []
    for j in range(SSM_RANK // 2):
        xp = xb[:, j * LANES:(j + 1) * LANES]
        ys = []
        for r_ in (2 * j, 2 * j + 1):
            diff = a_col[:, r_:r_ + 1] - a_row[r_:r_ + 1, :]
            mm = cb * jnp.exp(jnp.where(tri, diff, NEG_INF)) * dt_row[r_:r_ + 1, :]
            ys.append(_dot(mm.astype(BF16), xp))
        y_parts.append(jnp.where(lane_lo, ys[0], ys[1]))
    y_in = jnp.concatenate(y_parts, axis=1)

    state = state_ref[...]
    y_st = _dot(cm.astype(BF16), state.astype(BF16)) * _expand_heads(jnp.exp(a_col), L)
    w_col = jnp.exp(a_end - a_col) * dt
    xw = (x * _expand_heads(w_col, L)).astype(BF16)
    state_ref[...] = state * _expand_heads(jnp.exp(a_end), 1) + _dot(bm.T.astype(BF16), xw)

    y = y_in + y_st + dexp_ref[...] * x
    o_ref[...] = (y * _silu(z_ref[...].astype(F32))).astype(o_ref.dtype)


def ssd_mixer(proj, conv_w, conv_b, dt_bias, a_log, d_skip, *, batch, seq, offs):
    m = proj.shape[0]
    L = SSD_CHUNK
    nc = seq // L
    G = SSM_GROUPS
    xo, bo, co, dto = (offs[k] for k in ("xs", "B", "C", "dt"))

    def pad_heads(p):
        return jnp.pad(p.reshape(G, 1, SSM_RANK), ((0, 0), (0, 0), (0, LANES - SSM_RANK)))

    d_exp = jnp.repeat(d_skip, SSM_HEAD_DIM).reshape(G, 1, SSM_GX)
    cb2 = conv_b.reshape(1, -1)
    row = lambda b, g, c: b * nc + c
    return pl.pallas_call(
        _ssd_kernel,
        grid=(batch, G, nc),
        in_specs=[pl.BlockSpec((L, SSM_GX), lambda b, g, c: (row(b, g, c), xo // SSM_GX + g)),
                  pl.BlockSpec((L, LANES), lambda b, g, c: (row(b, g, c), bo // LANES + g)),
                  pl.BlockSpec((L, LANES), lambda b, g, c: (row(b, g, c), co // LANES + g)),
                  pl.BlockSpec((L, LANES), lambda b, g, c: (row(b, g, c), dto // LANES + g)),
                  pl.BlockSpec((L, SSM_GX), lambda b, g, c: (row(b, g, c), g)),
                  pl.BlockSpec((CONV_W, SSM_GX), lambda b, g, c: (0, g)),
                  pl.BlockSpec((CONV_W, LANES), lambda b, g, c: (0, W_TOK // LANES + g)),
                  pl.BlockSpec((CONV_W, LANES), lambda b, g, c: (0, (W_TOK + G * SSM_STATE) // LANES + g)),
                  pl.BlockSpec((1, SSM_GX), lambda b, g, c: (0, g)),
                  pl.BlockSpec((1, LANES), lambda b, g, c: (0, W_TOK // LANES + g)),
                  pl.BlockSpec((1, LANES), lambda b, g, c: (0, (W_TOK + G * SSM_STATE) // LANES + g)),
                  pl.BlockSpec((None, 1, LANES), lambda b, g, c: (g, 0, 0)),
                  pl.BlockSpec((None, 1, LANES), lambda b, g, c: (g, 0, 0)),
                  pl.BlockSpec((None, 1, SSM_GX), lambda b, g, c: (g, 0, 0))],
        out_specs=pl.BlockSpec((L, SSM_GX), lambda b, g, c: (row(b, g, c), g)),
        out_shape=jax.ShapeDtypeStruct((m, W_TOK), BF16),
        scratch_shapes=[pltpu.VMEM((SSM_STATE, SSM_GX), F32),
                        pltpu.VMEM((8, SSM_GX), F32),
                        pltpu.VMEM((8, LANES), F32),
                        pltpu.VMEM((8, LANES), F32)],
        compiler_params=_cparams(("parallel", "parallel", "arbitrary")),
        name="ssd_mixer",
    )(proj, proj, proj, proj, proj, conv_w, conv_w, conv_w, cb2, cb2, cb2,
      pad_heads(dt_bias), pad_heads(a_log), d_exp)


def _rope_table_kernel(pos_ref, inv_ref, cos_ref, sin_ref):
    ang = pos_ref[...].astype(F32) * inv_ref[...]
    lane = lax.broadcasted_iota(jnp.int32, (1, NSA_HEAD_DIM), 1)
    cos_ref[...] = jnp.cos(ang)
    sin_ref[...] = jnp.where(lane < NSA_HEAD_DIM // 2, -1.0, 1.0) * jnp.sin(ang)


def rope_tables(positions, *, tm):
    m = positions.size
    inv = ROPE_THETA ** (-jnp.arange(0, NSA_HEAD_DIM, 2, dtype=F32) / NSA_HEAD_DIM)
    inv2 = jnp.concatenate([inv, inv]).reshape(1, NSA_HEAD_DIM)
    return pl.pallas_call(
        _rope_table_kernel,
        grid=(m // tm,),
        in_specs=[pl.BlockSpec((tm, 1), lambda i: (i, 0)),
                  pl.BlockSpec((1, NSA_HEAD_DIM), lambda i: (0, 0))],
        out_specs=[pl.BlockSpec((tm, NSA_HEAD_DIM), lambda i: (i, 0))] * 2,
        out_shape=[jax.ShapeDtypeStruct((m, NSA_HEAD_DIM), F32)] * 2,
        compiler_params=_cparams(("parallel",)),
        name="rope_tables",
    )(positions.reshape(m, 1), inv2)


def _nsa_prep_kernel(q_ref, kv_ref, cos_ref, sin_ref, qn_ref, kn_ref,
                     qt_ref, cmp_ref, ks_ref, vst_ref, kw_ref, vwt_ref, *, tm):
    d = NSA_HEAD_DIM
    cos = cos_ref[...]
    sin = sin_ref[...]

    def norm_rope(x, gain, scale):
        x = x.astype(F32)
        x = x * lax.rsqrt(jnp.mean(x * x, axis=-1, keepdims=True) + EPS) * gain
        return (x * cos + pltpu.roll(x, d // 2, 1) * sin) * scale

    for h in range(NSA_KV_HEADS):
        for g in range(NSA_GROUP):
            sl = slice((h * NSA_GROUP + g) * d, (h * NSA_GROUP + g + 1) * d)
            q = norm_rope(q_ref[:, sl], qn_ref[...], d ** -0.5 * LOG2E)
            qt_ref[h, :, g * tm:(g + 1) * tm] = q.T.astype(qt_ref.dtype)
    for h in range(NSA_KV_HEADS):
        ksl = lambda br: slice((2 * br) * NSA_KV_W + h * d, (2 * br) * NSA_KV_W + (h + 1) * d)
        vsl = lambda br: slice((2 * br + 1) * NSA_KV_W + h * d, (2 * br + 1) * NSA_KV_W + (h + 1) * d)
        cmp_ref[:, h * d:(h + 1) * d] = norm_rope(kv_ref[:, ksl(0)], kn_ref[...], 1.0).astype(cmp_ref.dtype)
        cmp_ref[:, NSA_KV_W + h * d:NSA_KV_W + (h + 1) * d] = kv_ref[:, vsl(0)].astype(cmp_ref.dtype)
        for br, k_out, vt_out in ((1, ks_ref, vst_ref), (2, kw_ref, vwt_ref)):
            k_out[:, h * d:(h + 1) * d] = norm_rope(kv_ref[:, ksl(br)], kn_ref[...], 1.0).astype(k_out.dtype)
            vt_out[h] = kv_ref[:, vsl(br)].astype(F32).T.astype(vt_out.dtype)


def nsa_prep(proj, cos, sin, qn, kn, *, offs, tm):
    m = proj.shape[0]
    assert offs["q"] == 0 and offs["kv"] == W_TOK
    d = NSA_HEAD_DIM
    row = lambda w: pl.BlockSpec((tm, w), lambda i: (i, 0))
    vt = pl.BlockSpec((NSA_KV_HEADS, d, tm), lambda i: (0, 0, i))
    return pl.pallas_call(
        functools.partial(_nsa_prep_kernel, tm=tm),
        grid=(m // tm,),
        in_specs=[pl.BlockSpec((tm, W_TOK), lambda i: (i, 0)),
                  pl.BlockSpec((tm, W_TOK), lambda i: (i, 1)),
                  row(d), row(d),
                  pl.BlockSpec((1, d), lambda i: (0, 0)),
                  pl.BlockSpec((1, d), lambda i: (0, 0))],
        out_specs=[pl.BlockSpec((None, NSA_KV_HEADS, d, NSA_GROUP * tm), lambda i: (i, 0, 0, 0)),
                   row(2 * NSA_KV_W), row(NSA_KV_W), vt, row(NSA_KV_W), vt],
        out_shape=[jax.ShapeDtypeStruct((m // tm, NSA_KV_HEADS, d, NSA_GROUP * tm), BF16),
                   jax.ShapeDtypeStruct((m, 2 * NSA_KV_W), BF16),
                   jax.ShapeDtypeStruct((m, NSA_KV_W), BF16),
                   jax.ShapeDtypeStruct((NSA_KV_HEADS, d, m), BF16),
                   jax.ShapeDtypeStruct((m, NSA_KV_W), BF16),
                   jax.ShapeDtypeStruct((NSA_KV_HEADS, d, m), BF16)],
        compiler_params=_cparams(("parallel",)),
        name="nsa_prep",
    )(proj, proj, cos, sin, qn.reshape(1, -1), kn.reshape(1, -1))


def _nsa_compress_kernel(r_ref, pos_ref, kw1_ref, kw2_ref, vw1_ref, vw2_ref, kc_ref, vct_ref):
    d = NSA_HEAD_DIM
    half = CMP_STRIDE * d
    n = r_ref.shape[0]
    posb = jnp.broadcast_to(pos_ref[...], (8, CMP_BLOCK * d)).astype(BF16)
    for off, w1_ref, w2_ref, out, transposed in ((0, kw1_ref, kw2_ref, kc_ref, False),
                                                 (NSA_KV_W, vw1_ref, vw2_ref, vct_ref, True)):
        pos_term = _dot(posb, w1_ref[...])[0:1]
        for h in range(NSA_KV_HEADS):
            cat = jnp.concatenate(
                [r_ref[:, tt * 2 * NSA_KV_W + off + h * d: tt * 2 * NSA_KV_W + off + (h + 1) * d]
                 for tt in range(CMP_STRIDE)], axis=1)
            ha = _dot(cat, w1_ref[0:half, :])
            hb = _dot(cat, w1_ref[half:2 * half, :])
            hsum = ha + pltpu.roll(hb, n - 1, 0) + pos_term
            res = _dot(_silu(hsum).astype(BF16), w2_ref[...])
            out[h] = (res.T if transposed else res).astype(out.dtype)


def nsa_compress(cmp_in, cmp_pos, kw1, kw2, vw1, vw2, *, batch, seq):
    n = seq // CMP_STRIDE
    wide = CMP_STRIDE * 2 * NSA_KV_W
    d = NSA_HEAD_DIM
    r = cmp_in.reshape(batch * n, wide)
    full = lambda shape: pl.BlockSpec(shape, lambda b: (0,) * len(shape))
    return pl.pallas_call(
        _nsa_compress_kernel,
        grid=(batch,),
        in_specs=[pl.BlockSpec((n, wide), lambda b: (b, 0)),
                  full((1, CMP_BLOCK * d)), full((CMP_BLOCK * d, d)), full((d, d)),
                  full((CMP_BLOCK * d, d)), full((d, d))],
        out_specs=[pl.BlockSpec((None, NSA_KV_HEADS, n, d), lambda b: (b, 0, 0, 0)),
                   pl.BlockSpec((None, NSA_KV_HEADS, d, n), lambda b: (b, 0, 0, 0))],
        out_shape=[jax.ShapeDtypeStruct((batch, NSA_KV_HEADS, n, d), BF16),
                   jax.ShapeDtypeStruct((batch, NSA_KV_HEADS, d, n), BF16)],
        compiler_params=_cparams(("parallel",)),
        name="nsa_compress",
    )(r, cmp_pos.reshape(1, CMP_BLOCK * d), kw1.astype(BF16), kw2.astype(BF16), vw1.astype(BF16), vw2.astype(BF16))


def _store_gated_t(o_t, g, gate_t, o_ref, branch):
    lane = branch * NSA_GROUP + g
    gate = jax.nn.sigmoid(gate_t[lane:lane + 1, :])
    o_ref[:, g * NSA_HEAD_DIM:(g + 1) * NSA_HEAD_DIM] = (o_t * gate).T.astype(o_ref.dtype)


def _nsa_cmp_kernel(qt_ref, kc_ref, vct_ref, gl_ref, o_ref, selt_ref, *, tq, n_sel):
    i = pl.program_id(2)
    nc = kc_ref.shape[0]
    t = i * tq + lax.broadcasted_iota(jnp.int32, (1, tq), 1)
    cend = lax.broadcasted_iota(jnp.int32, (nc, 1), 0) * CMP_STRIDE + (CMP_BLOCK - 1)
    cmask = cend <= t
    any_vis = jnp.where(t >= CMP_BLOCK - 1, 1.0, 0.0)
    gate_t = gl_ref[...].astype(F32).T
    s_all = _dot(kc_ref[...], qt_ref[...])
    vct = vct_ref[...]
    psum = jnp.zeros((nc, tq), F32)
    for g in range(NSA_GROUP):
        s = jnp.where(cmask, s_all[:, g * tq:(g + 1) * tq], NEG_INF)
        e = jnp.exp2(s - jnp.max(s, axis=0, keepdims=True))
        p = e * (any_vis / jnp.sum(e, axis=0, keepdims=True))
        psum = psum + p
        _store_gated_t(_dot(vct, p.astype(BF16)), g, gate_t, o_ref, 0)

    js = lax.broadcasted_iota(jnp.int32, (n_sel, nc), 0) * SEL_BLOCK
    cs = lax.broadcasted_iota(jnp.int32, (n_sel, nc), 1) * CMP_STRIDE
    agg_t = jnp.where((cs < js + SEL_BLOCK) & (cs + CMP_BLOCK - 1 >= js), 1.0, 0.0).astype(BF16)
    hi, mid, lo = _split3(psum)
    imp = _dot(agg_t, hi) + _dot(agg_t, mid) + _dot(agg_t, lo)
    j = lax.broadcasted_iota(jnp.int32, (n_sel, 1), 0)
    bt = t // SEL_BLOCK
    forced = (j == 0) | (j == bt) | (j == bt - 1)
    sc = jnp.where(forced, SEL_BIG, jnp.where(j <= bt, imp, -SEL_BIG))
    rowi = lax.broadcasted_iota(jnp.int32, (n_sel, tq), 0).astype(F32)
    sel = jnp.zeros((n_sel, tq), F32)
    for _ in range(min(SEL_TOPK, n_sel)):
        mx = jnp.max(sc, axis=0, keepdims=True)
        idx = jnp.min(jnp.where(sc == mx, rowi, float(n_sel)), axis=0, keepdims=True)
        pick = rowi == idx
        sel = jnp.where(pick, 1.0, sel)
        sc = jnp.where(pick, -jnp.inf, sc)
    selt_ref[...] = sel


def nsa_cmp_attn(q_t, kc, vct, proj, *, batch, seq, gl_off, tq):
    m = batch * seq
    nq = seq // tq
    n_cmp = kc.shape[2]
    n_sel = seq // SEL_BLOCK
    d = NSA_HEAD_DIM
    gw = NSA_GROUP * d
    glb = gl_off // LANES
    return pl.pallas_call(
        functools.partial(_nsa_cmp_kernel, tq=tq, n_sel=n_sel),
        grid=(batch, NSA_KV_HEADS, nq),
        in_specs=[pl.BlockSpec((None, None, d, NSA_GROUP * tq), lambda b, h, i: (b * nq + i, h, 0, 0)),
                  pl.BlockSpec((None, None, n_cmp, d), lambda b, h, i: (b, h, 0, 0)),
                  pl.BlockSpec((None, None, d, n_cmp), lambda b, h, i: (b, h, 0, 0)),
                  pl.BlockSpec((tq, LANES), lambda b, h, i: (b * nq + i, glb + h))],
        out_specs=[pl.BlockSpec((tq, gw), lambda b, h, i: (b * nq + i, h)),
                   pl.BlockSpec((None, None, n_sel, tq), lambda b, h, i: (b, h, 0, i))],
        out_shape=[jax.ShapeDtypeStruct((m, W_TOK), BF16),
                   jax.ShapeDtypeStruct((batch, NSA_KV_HEADS, n_sel, seq), F32)],
        compiler_params=_cparams(("parallel", "parallel", "parallel")),
        name="nsa_cmp_attn",
    )(q_t, kc, vct, proj)


def _nsa_flash_kernel(qi_ref, ki_ref, first_ref, last_ref, qt_ref, k_ref, vt_ref, selt_ref, gl_ref, o_ref,
                      m_ref, l_ref, acc_ref, *, tq, tk, branch):
    step = pl.program_id(2)
    qi = qi_ref[step]
    ki = ki_ref[step]

    @pl.when(first_ref[step] == 1)
    def _():
        m_ref[...] = jnp.full_like(m_ref, NEG_INF)
        l_ref[...] = jnp.zeros_like(l_ref)
        acc_ref[...] = jnp.zeros_like(acc_ref)

    t = qi * tq + lax.broadcasted_iota(jnp.int32, (1, tq), 1)
    kp = ki * tk + lax.broadcasted_iota(jnp.int32, (tk, 1), 0)
    if branch == 1:
        per_tile = tk // SEL_BLOCK
        per_load = 8 // per_tile
        rows8 = selt_ref[pl.ds(pl.multiple_of((ki // per_load) * 8, 8), 8), :]
        sub = ki % per_load
        rows = rows8[0:per_tile]
        for u in range(1, per_load):
            rows = jnp.where(sub == u, rows8[u * per_tile:(u + 1) * per_tile], rows)
        selm = jnp.concatenate([jnp.broadcast_to(rows[u:u + 1], (SEL_BLOCK, tq)) for u in range(per_tile)], axis=0)
        mask = (selm > 0.5) & (kp <= t)
    else:
        mask = (kp <= t) & (kp > t - WINDOW)
    bias = jnp.where(mask, 0.0, -jnp.inf)
    s_all = _dot(k_ref[...], qt_ref[...])
    vt = vt_ref[...]
    m_all = m_ref[...]
    l_all = l_ref[...]
    m_rows, l_rows = [], []
    for g in range(NSA_GROUP):
        cols = slice(g * tq, (g + 1) * tq)
        s = s_all[:, cols] + bias
        m_old = m_all[g:g + 1, :]
        m_new = jnp.maximum(m_old, jnp.max(s, axis=0, keepdims=True))
        alpha = jnp.exp2(m_old - m_new)
        p = jnp.exp2(s - m_new)
        l_rows.append(alpha * l_all[g:g + 1, :] + jnp.sum(p, axis=0, keepdims=True))
        m_rows.append(m_new)
        acc_ref[:, cols] = alpha * acc_ref[:, cols] + _dot(vt, p.astype(BF16))
    m_ref[...] = jnp.concatenate(m_rows, axis=0)
    l_ref[...] = jnp.concatenate(l_rows, axis=0)

    @pl.when(last_ref[step] == 1)
    def _():
        gate_t = gl_ref[...].astype(F32).T
        for g in range(NSA_GROUP):
            o_t = acc_ref[:, g * tq:(g + 1) * tq] / l_ref[g:g + 1, :]
            _store_gated_t(o_t, g, gate_t, o_ref, branch)


def nsa_flash_attn(q_t, k, v_t, sel_t, proj, *, batch, seq, gl_off, branch, tq, tk):
    m = batch * seq
    nq = seq // tq
    nk = seq // tk
    d = NSA_HEAD_DIM
    gw = NSA_GROUP * d
    glb = gl_off // LANES
    n_sel = seq // SEL_BLOCK
    assert 8 % (tk // SEL_BLOCK) == 0 and n_sel % 8 == 0
    steps = []
    for i in range(nq):
        hi_k = ((i + 1) * tq - 1) // tk
        lo_k = 0 if branch == 1 else max(0, (i * tq - WINDOW + 1) // tk)
        for kk in range(lo_k, hi_k + 1):
            steps.append((i, kk, int(kk == lo_k), int(kk == hi_k)))
    tabs = [jnp.asarray(np.array([s_[c] for s_ in steps], np.int32)) for c in range(4)]
    grid_spec = pltpu.PrefetchScalarGridSpec(
        num_scalar_prefetch=4,
        grid=(batch, NSA_KV_HEADS, len(steps)),
        in_specs=[pl.BlockSpec((None, None, d, NSA_GROUP * tq), lambda b, h, s, qi, ki, fi, la: (b * nq + qi[s], h, 0, 0)),
                  pl.BlockSpec((tk, d), lambda b, h, s, qi, ki, fi, la: (b * nk + ki[s], h)),
                  pl.BlockSpec((None, d, tk), lambda b, h, s, qi, ki, fi, la: (h, 0, b * nk + ki[s])),
                  pl.BlockSpec((None, None, n_sel, tq), lambda b, h, s, qi, ki, fi, la: (b, h, 0, qi[s])),
                  pl.BlockSpec((tq, LANES), lambda b, h, s, qi, ki, fi, la: (b * nq + qi[s], glb + h))],
        out_specs=pl.BlockSpec((tq, gw), lambda b, h, s, qi, ki, fi, la: (b * nq + qi[s], h)),
        scratch_shapes=[pltpu.VMEM((NSA_GROUP, tq), F32),
                        pltpu.VMEM((NSA_GROUP, tq), F32),
                        pltpu.VMEM((d, NSA_GROUP * tq), F32)])
    return pl.pallas_call(
        functools.partial(_nsa_flash_kernel, tq=tq, tk=tk, branch=branch),
        grid_spec=grid_spec,
        out_shape=jax.ShapeDtypeStruct((m, W_TOK), BF16),
        compiler_params=_cparams(("parallel", "parallel", "arbitrary")),
        name="nsa_sel_attn" if branch == 1 else "nsa_win_attn",
    )(*tabs, q_t, k, v_t, sel_t, proj)


def _nsa_combine_kernel(a_ref, b_ref, c_ref, gate_ref, o_ref):
    o = a_ref[...].astype(F32) + b_ref[...].astype(F32) + c_ref[...].astype(F32)
    o_ref[...] = (o * _silu(gate_ref[...].astype(F32))).astype(o_ref.dtype)


def nsa_combine(o_cmp, o_sel, o_win, proj, *, gate_off, tm):
    m = o_cmp.shape[0]
    gb = gate_off // W_TOK
    assert gate_off % W_TOK == 0
    blk = lambda cb: pl.BlockSpec((tm, W_TOK), lambda i: (i, cb))
    return pl.pallas_call(
        _nsa_combine_kernel,
        grid=(m // tm,),
        in_specs=[blk(0), blk(0), blk(0), blk(gb)],
        out_specs=blk(0),
        out_shape=jax.ShapeDtypeStruct((m, W_TOK), BF16),
        compiler_params=_cparams(("parallel",)),
        name="nsa_combine",
    )(o_cmp, o_sel, o_win, proj)


def _nsa_layout(w_in):
    kv_end = W_TOK + 6 * NSA_KV_W
    gl_end = kv_end + 3 * NSA_HEADS
    k = w_in.shape[0]
    glw = w_in[:, kv_end:gl_end].reshape(k, NSA_KV_HEADS, NSA_GROUP, 3)
    glw = glw.transpose(0, 1, 3, 2).reshape(k, NSA_KV_HEADS, 3 * NSA_GROUP)
    glw = jnp.pad(glw, ((0, 0), (0, 0), (0, LANES - 3 * NSA_GROUP))).reshape(k, NSA_KV_HEADS * LANES)
    pad_w = 2 * W_TOK - kv_end - NSA_KV_HEADS * LANES
    w = jnp.concatenate([w_in[:, :kv_end], glw, jnp.zeros((k, pad_w), w_in.dtype), w_in[:, gl_end:]], axis=1)
    offs = {"q": 0, "kv": W_TOK, "gl": kv_end, "gate": 2 * W_TOK, "memq": 3 * W_TOK}
    return w.astype(BF16), offs


def _ssd_layout(w_in):
    k = w_in.shape[0]
    xbc_end = W_TOK + W_TOK + 2 * SSM_GROUPS * SSM_STATE
    dt_end = xbc_end + SSM_HEADS
    dtw = w_in[:, xbc_end:dt_end].reshape(k, SSM_GROUPS, SSM_RANK)
    dtw = jnp.pad(dtw, ((0, 0), (0, 0), (0, LANES - SSM_RANK))).reshape(k, SSM_GROUPS * LANES)
    w = jnp.concatenate([w_in[:, :xbc_end], w_in[:, dt_end:], dtw], axis=1)
    offs = {"z": 0, "xs": W_TOK, "B": 2 * W_TOK, "C": 2 * W_TOK + SSM_GROUPS * SSM_STATE,
            "memq": xbc_end, "dt": xbc_end + 2 * W_MEM}
    return w.astype(BF16), offs


def nsa_mixer(proj, positions, qn, kn, cmp_pos, kw1, kw2, vw1, vw2, *, batch, seq, offs):
    cos, sin = rope_tables(positions, tm=min(1024, batch * seq))
    tq = 256
    common = dict(batch=batch, seq=seq, gl_off=offs["gl"])
    q_t, cmp_in, k_s, vt_s, k_w, vt_w = nsa_prep(proj, cos, sin, qn, kn, offs=offs, tm=tq)
    kc, vct = nsa_compress(cmp_in, cmp_pos, kw1, kw2, vw1, vw2, batch=batch, seq=seq)
    o_cmp, sel_t = nsa_cmp_attn(q_t, kc, vct, proj, tq=tq, **common)
    o_sel = nsa_flash_attn(q_t, k_s, vt_s, sel_t, proj, branch=1, tq=tq, tk=256, **common)
    o_win = nsa_flash_attn(q_t, k_w, vt_w, sel_t, proj, branch=2, tq=tq, tk=256, **common)
    return nsa_combine(o_cmp, o_sel, o_win, proj, gate_off=offs["gate"], tm=256)


def _layer(x2, mem2, mem_norm, norm, w_in_b, w_out, mem_wkv, mem_qn, mem_kn, mixer, memq_off, *, seq, gnorm=None):
    proj = norm_matmul(x2, norm, w_in_b, tm=1024 if x2.shape[0] % 1024 == 0 else x2.shape[0], tn=512, out_dtype=BF16)
    kv = norm_matmul(mem2, mem_norm, mem_wkv.astype(BF16), tm=mem2.shape[0] if mem2.shape[0] < 512 else 512,
                     tn=512, out_dtype=F32)
    y_mem = mem_attn(proj, kv, mem_qn, mem_kn, q_off=memq_off, seq=seq, tq=min(512, seq))
    y_tok = mixer(proj)
    w_tok = w_out[:W_TOK].astype(BF16)
    w_mem = w_out[W_TOK:].astype(BF16)
    gain = gnorm if gnorm is not None else jnp.ones((W_TOK,), F32)
    return out_proj(y_tok, y_mem, w_tok, w_mem, x2, gain, norm_tok=gnorm is not None,
                    tm=512 if x2.shape[0] % 512 == 0 else x2.shape[0], tn=512)


def kernel(x, mem, positions, mem_norm, l0_norm, l0_w_in, l0_w_out, l0_mem_wkv, l0_mem_qnorm, l0_mem_knorm, l0_pool_w, l0_pool_scale, l1_norm, l1_w_in, l1_w_out, l1_mem_wkv, l1_mem_qnorm, l1_mem_knorm, l1_qnorm, l1_knorm, l1_cmp_pos, l1_cmp_k_w1, l1_cmp_k_w2, l1_cmp_v_w1, l1_cmp_v_w2, l2_norm, l2_w_in, l2_w_out, l2_mem_wkv, l2_mem_qnorm, l2_mem_knorm, l2_conv_w, l2_conv_b, l2_dt_bias, l2_A_log, l2_D, l2_gnorm, l3_norm, l3_w_in, l3_w_out, l3_mem_wkv, l3_mem_qnorm, l3_mem_knorm, l3_v_norm, l3_sgu_w, l3_sgu_b):
    batch, seq, d = x.shape
    x2 = x.reshape(batch * seq, d)
    mem2 = mem.reshape(batch * MEM_TOKENS, d)

    x2 = _layer(x2, mem2, mem_norm, l0_norm, l0_w_in.astype(BF16), l0_w_out, l0_mem_wkv, l0_mem_qnorm, l0_mem_knorm,
                lambda p: pool_mixer(p, l0_pool_w, l0_pool_scale, seq=seq, tm=256), 2 * W_TOK, seq=seq)

    w1, offs1 = _nsa_layout(l1_w_in)
    x2 = _layer(x2, mem2, mem_norm, l1_norm, w1, l1_w_out, l1_mem_wkv, l1_mem_qnorm, l1_mem_knorm,
                lambda p: nsa_mixer(p, positions, l1_qnorm, l1_knorm, l1_cmp_pos, l1_cmp_k_w1, l1_cmp_k_w2,
                                    l1_cmp_v_w1, l1_cmp_v_w2, batch=batch, seq=seq, offs=offs1),
                offs1["memq"], seq=seq)

    w2, offs2 = _ssd_layout(l2_w_in)
    x2 = _layer(x2, mem2, mem_norm, l2_norm, w2, l2_w_out, l2_mem_wkv, l2_mem_qnorm, l2_mem_knorm,
                lambda p: ssd_mixer(p, l2_conv_w, l2_conv_b, l2_dt_bias, l2_A_log, l2_D, batch=batch, seq=seq, offs=offs2),
                offs2["memq"], seq=seq, gnorm=l2_gnorm)

    x2 = _layer(x2, mem2, mem_norm, l3_norm, l3_w_in.astype(BF16), l3_w_out, l3_mem_wkv, l3_mem_qnorm, l3_mem_knorm,
                lambda p: sgu_mixer(p, l3_v_norm, l3_sgu_w, l3_sgu_b, tm=256), 3 * W_TOK, seq=seq)
    return x2.reshape(batch, seq, d)
```

```python
import functools
import math

import numpy as np
import jax
import jax.numpy as jnp
from jax import lax
from jax.experimental import pallas as pl
from jax.experimental.pallas import tpu as pltpu

F32 = jnp.float32
BF16 = jnp.bfloat16

D_MODEL = 2048
EPS = 1e-6
ROPE_THETA = 10000.0
NEG_INF = -1e30
SEL_BIG = 1e9
LOG2E = 1.0 / math.log(2.0)

W_TOK = 2 * D_MODEL
MEM_TOKENS = 256
MEM_HEADS = 4
MEM_HEAD_DIM = D_MODEL // 8
W_MEM = MEM_HEADS * MEM_HEAD_DIM

POOL_GROUPS = 4
POOL_GW = W_TOK // POOL_GROUPS
POOL_HALO = 16

NSA_HEADS = 32
NSA_KV_HEADS = 4
NSA_GROUP = NSA_HEADS // NSA_KV_HEADS
NSA_HEAD_DIM = 128
NSA_KV_W = NSA_KV_HEADS * NSA_HEAD_DIM
CMP_BLOCK = 32
CMP_STRIDE = 16
SEL_BLOCK = 64
SEL_TOPK = 16
WINDOW = 512

SSM_HEAD_DIM = 64
SSM_HEADS = W_TOK // SSM_HEAD_DIM
SSM_GROUPS = 8
SSM_RANK = SSM_HEADS // SSM_GROUPS
SSM_STATE = 128
CONV_W = 4
SSD_CHUNK = 256
SSM_GX = W_TOK // SSM_GROUPS
SSD_HALO = 16

SGU_CHUNK = 128
SGU_GROUPS = 8
SGU_GW = W_TOK // SGU_GROUPS

LANES = 128
VMEM_LIMIT = 56 * 1024 * 1024


def _cparams(sem):
    return pltpu.CompilerParams(dimension_semantics=sem, vmem_limit_bytes=VMEM_LIMIT)


def _tile(n, pref):
    t = pref
    while n % t:
        t //= 2
    return t


def _silu(x):
    return x * jax.nn.sigmoid(x)


def _gelu_exact(x):
    return 0.5 * x * (1.0 + lax.erf(x * (1.0 / math.sqrt(2.0))))


def _dot(a, b):
    return jnp.dot(a, b, preferred_element_type=F32)


def _dot_nt(a, b):
    return lax.dot_general(a, b, (((1,), (1,)), ((), ())), preferred_element_type=F32)


def _split3(x):
    hi = x.astype(BF16)
    r1 = x - hi.astype(F32)
    mid = r1.astype(BF16)
    lo = (r1 - mid.astype(F32)).astype(BF16)
    return hi, mid, lo


def _norm_matmul_kernel(x_ref, g_ref, w_ref, o_ref, h_ref):
    @pl.when(pl.program_id(1) == 0)
    def _():
        x = x_ref[...].astype(F32)
        ms = jnp.mean(x * x, axis=-1, keepdims=True)
        h_ref[...] = (x * lax.rsqrt(ms + EPS) * g_ref[...]).astype(BF16)

    o_ref[...] = _dot(h_ref[...], w_ref[...]).astype(o_ref.dtype)


def norm_matmul(x, g, w, *, tm, tn, out_dtype):
    m, k = x.shape
    n = w.shape[1]
    return pl.pallas_call(
        _norm_matmul_kernel,
        grid=(m // tm, n // tn),
        in_specs=[pl.BlockSpec((tm, k), lambda i, j: (i, 0)),
                  pl.BlockSpec((1, k), lambda i, j: (0, 0)),
                  pl.BlockSpec((k, tn), lambda i, j: (0, j))],
        out_specs=pl.BlockSpec((tm, tn), lambda i, j: (i, j)),
        out_shape=jax.ShapeDtypeStruct((m, n), out_dtype),
        scratch_shapes=[pltpu.VMEM((tm, k), BF16)],
        compiler_params=_cparams(("parallel", "arbitrary")),
        name="norm_matmul",
    )(x, g.reshape(1, k), w)


def _out_proj_kernel(yt_ref, ym_ref, wt_ref, wm_ref, x_ref, g_ref, o_ref, *scratch, norm_tok):
    if norm_tok:
        (h_ref,) = scratch

        @pl.when(pl.program_id(1) == 0)
        def _():
            y = yt_ref[...].astype(F32)
            ms = jnp.mean(y * y, axis=-1, keepdims=True)
            h_ref[...] = (y * lax.rsqrt(ms + EPS) * g_ref[...]).astype(BF16)

        yt = h_ref[...]
    else:
        yt = yt_ref[...]
    acc = _dot(yt, wt_ref[...]) + _dot(ym_ref[...], wm_ref[...])
    o_ref[...] = x_ref[...] + acc


def out_proj(y_tok, y_mem, w_tok, w_mem, x, gain, *, norm_tok, tm, tn):
    m, kt = y_tok.shape
    km = y_mem.shape[1]
    n = w_tok.shape[1]
    return pl.pallas_call(
        functools.partial(_out_proj_kernel, norm_tok=norm_tok),
        grid=(m // tm, n // tn),
        in_specs=[pl.BlockSpec((tm, kt), lambda i, j: (i, 0)),
                  pl.BlockSpec((tm, km), lambda i, j: (i, 0)),
                  pl.BlockSpec((kt, tn), lambda i, j: (0, j)),
                  pl.BlockSpec((km, tn), lambda i, j: (0, j)),
                  pl.BlockSpec((tm, tn), lambda i, j: (i, j)),
                  pl.BlockSpec((1, kt), lambda i, j: (0, 0))],
        out_specs=pl.BlockSpec((tm, tn), lambda i, j: (i, j)),
        out_shape=jax.ShapeDtypeStruct((m, n), F32),
        scratch_shapes=[pltpu.VMEM((tm, kt), BF16)] if norm_tok else [],
        compiler_params=_cparams(("parallel", "arbitrary")),
        name="out_proj",
    )(y_tok, y_mem, w_tok, w_mem, x, gain.reshape(1, kt))


def _mem_attn_kernel(q_ref, gate_ref, kv_ref, qn_ref, kn_ref, o_ref):
    hd = MEM_HEAD_DIM
    scale = hd ** -0.5
    for h in range(MEM_HEADS):
        q = q_ref[:, h * hd:(h + 1) * hd].astype(F32)
        q = q * lax.rsqrt(jnp.mean(q * q, axis=-1, keepdims=True) + EPS) * qn_ref[...]
        k = kv_ref[:, h * hd:(h + 1) * hd].astype(F32)
        k = k * lax.rsqrt(jnp.mean(k * k, axis=-1, keepdims=True) + EPS) * kn_ref[...]
        v = kv_ref[:, W_MEM + h * hd:W_MEM + (h + 1) * hd].astype(BF16)
        s = _dot_nt((q * scale).astype(BF16), k.astype(BF16))
        e = jnp.exp(s - jnp.max(s, axis=-1, keepdims=True))
        o = _dot(e.astype(BF16), v) / jnp.sum(e, axis=-1, keepdims=True)
        gate = gate_ref[:, h * hd:(h + 1) * hd].astype(F32)
        o_ref[:, h * hd:(h + 1) * hd] = (o * _silu(gate)).astype(o_ref.dtype)


def mem_attn(proj, kv, qn, kn, *, q_off, seq, tq):
    m = proj.shape[0]
    nq = seq // tq
    qb = q_off // W_MEM
    return pl.pallas_call(
        _mem_attn_kernel,
        grid=(m // tq,),
        in_specs=[pl.BlockSpec((tq, W_MEM), lambda i: (i, qb)),
                  pl.BlockSpec((tq, W_MEM), lambda i: (i, qb + 1)),
                  pl.BlockSpec((MEM_TOKENS, 2 * W_MEM), lambda i: (i // nq, 0)),
                  pl.BlockSpec((1, MEM_HEAD_DIM), lambda i: (0, 0)),
                  pl.BlockSpec((1, MEM_HEAD_DIM), lambda i: (0, 0))],
        out_specs=pl.BlockSpec((tq, W_MEM), lambda i: (i, 0)),
        out_shape=jax.ShapeDtypeStruct((m, W_MEM), BF16),
        compiler_params=_cparams(("parallel",)),
        name="mem_attn",
    )(proj, proj, kv, qn.reshape(1, -1), kn.reshape(1, -1))


def _pool_kernel(v_ref, halo_ref, gate_ref, w_ref, scale_ref, o_ref, *, tm, seq):
    g = pl.program_id(0)
    i = pl.program_id(1)
    win = jnp.left_shift(2, g)
    t_seq = lax.rem(i * tm, seq)
    r = lax.broadcasted_iota(jnp.int32, (tm, 1), 0)
    c = lax.broadcasted_iota(jnp.int32, (1, tm), 1)
    a_main = jnp.where((c <= r) & (c > r - win), 1.0, 0.0).astype(BF16)
    ch = lax.broadcasted_iota(jnp.int32, (1, POOL_HALO), 1)
    a_halo = jnp.where((ch > r + POOL_HALO - win) & (t_seq > 0), 1.0, 0.0).astype(BF16)
    v = v_ref[...]
    wsum = _dot(a_main, v.astype(BF16)) + _dot(a_halo, halo_ref[...].astype(BF16))
    cnt = jnp.minimum(t_seq + r + 1, win).astype(F32)
    mix = wsum / cnt - v.astype(F32)
    out = _dot(mix.astype(BF16), w_ref[...]) * scale_ref[...]
    o_ref[...] = (out * _silu(gate_ref[...].astype(F32))).astype(o_ref.dtype)


def pool_mixer(proj, pool_w, pool_scale, *, seq, tm):
    m = proj.shape[0]
    hb = tm // POOL_HALO
    return pl.pallas_call(
        functools.partial(_pool_kernel, tm=tm, seq=seq),
        grid=(POOL_GROUPS, m // tm),
        in_specs=[pl.BlockSpec((tm, POOL_GW), lambda g, i: (i, g)),
                  pl.BlockSpec((POOL_HALO, POOL_GW), lambda g, i: (jnp.maximum(i * hb - 1, 0), g)),
                  pl.BlockSpec((tm, POOL_GW), lambda g, i: (i, POOL_GROUPS + g)),
                  pl.BlockSpec((None, POOL_GW, POOL_GW), lambda g, i: (g, 0, 0)),
                  pl.BlockSpec((1, POOL_GW), lambda g, i: (0, g))],
        out_specs=pl.BlockSpec((tm, POOL_GW), lambda g, i: (i, g)),
        out_shape=jax.ShapeDtypeStruct((m, W_TOK), BF16),
        compiler_params=_cparams(("parallel", "parallel")),
        name="pool_mixer",
    )(proj, proj, proj, pool_w.astype(BF16), pool_scale.reshape(1, W_TOK))


def _sgu_kernel(u_ref, v_ref, gate_ref, vn_ref, w_ref, bt_ref, o_ref, vs_ref, *, tm):
    ssq = jnp.zeros((tm, 1), F32)
    for g in range(SGU_GROUPS):
        sl = slice(g * SGU_GW, (g + 1) * SGU_GW)
        vg = _gelu_exact(v_ref[:, sl].astype(F32))
        vs_ref[:, sl] = vg
        ssq = ssq + jnp.sum(vg * vg, axis=-1, keepdims=True)
    inv = lax.rsqrt(ssq * (1.0 / W_TOK) + EPS)
    r = lax.broadcasted_iota(jnp.int32, (SGU_CHUNK, SGU_CHUNK), 0)
    c = lax.broadcasted_iota(jnp.int32, (SGU_CHUNK, SGU_CHUNK), 1)
    for g in range(SGU_GROUPS):
        sl = slice(g * SGU_GW, (g + 1) * SGU_GW)
        w = jnp.where(r >= c, w_ref[g], 0.0).astype(BF16)
        vn = (vs_ref[:, sl] * inv * vn_ref[:, sl]).astype(BF16)
        for ck in range(tm // SGU_CHUNK):
            rs = slice(ck * SGU_CHUNK, (ck + 1) * SGU_CHUNK)
            mixed = _dot(w, vn[rs]) + bt_ref[:, g:g + 1]
            u = _gelu_exact(u_ref[rs, sl].astype(F32))
            o_ref[rs, sl] = (u * mixed * _silu(gate_ref[rs, sl].astype(F32))).astype(o_ref.dtype)


def sgu_mixer(proj, v_norm, sgu_w, sgu_b, *, tm):
    m = proj.shape[0]
    return pl.pallas_call(
        functools.partial(_sgu_kernel, tm=tm),
        grid=(m // tm,),
        in_specs=[pl.BlockSpec((tm, W_TOK), lambda i: (i, 0)),
                  pl.BlockSpec((tm, W_TOK), lambda i: (i, 1)),
                  pl.BlockSpec((tm, W_TOK), lambda i: (i, 2)),
                  pl.BlockSpec((1, W_TOK), lambda i: (0, 0)),
                  pl.BlockSpec((SGU_GROUPS, SGU_CHUNK, SGU_CHUNK), lambda i: (0, 0, 0)),
                  pl.BlockSpec((SGU_CHUNK, SGU_GROUPS), lambda i: (0, 0))],
        out_specs=pl.BlockSpec((tm, W_TOK), lambda i: (i, 0)),
        out_shape=jax.ShapeDtypeStruct((m, W_TOK), BF16),
        scratch_shapes=[pltpu.VMEM((tm, W_TOK), F32)],
        compiler_params=_cparams(("parallel",)),
        name="sgu_mixer",
    )(proj, proj, proj, v_norm.reshape(1, W_TOK), sgu_w, sgu_b.T)


def _conv_silu(raw, prev, w, b):
    L = raw.shape[0]
    rr = lax.broadcasted_iota(jnp.int32, ((CONV_W - 1) * L, 1), 0)
    src = rr % L - (rr // L + 1)
    main = jnp.where(lax.broadcasted_iota(jnp.int32, (1, L), 1) == src, 1.0, 0.0).astype(BF16)
    shifted = _dot(main, raw)
    row8 = lax.broadcasted_iota(jnp.int32, (8, 1), 0)
    prev32 = prev.astype(F32)
    acc = raw.astype(F32) * w[CONV_W - 1:CONV_W, :] + b
    for k in range(1, CONV_W):
        blk = shifted[(k - 1) * L:k * L]
        head = blk[0:8] + jnp.where(row8 < k, pltpu.roll(prev32, k, 0)[0:8], 0.0)
        blk = jnp.concatenate([head, blk[8:]], axis=0)
        acc = acc + blk * w[CONV_W - 1 - k:CONV_W - k, :]
    return _silu(acc)


def _expand_heads(cols, n_rows):
    lane_head = lax.broadcasted_iota(jnp.int32, (1, SSM_GX), 1) // SSM_HEAD_DIM
    out = jnp.zeros((n_rows, SSM_GX), F32)
    for r in range(SSM_RANK):
        out = jnp.where(lane_head == r, cols[:, r:r + 1], out)
    return out


def _ssd_kernel(x_ref, b_ref, c_ref, dt_ref, z_ref, cwx_ref, cwb_ref, cwc_ref, cbx_ref, cbb_ref, cbc_ref,
                dtb_ref, alog_ref, dexp_ref, o_ref, state_ref, prev_ref):
    L = SSD_CHUNK
    ck = pl.program_id(2)

    @pl.when(ck == 0)
    def _():
        state_ref[...] = jnp.zeros_like(state_ref)
        prev_ref[...] = jnp.zeros_like(prev_ref)

    raw = jnp.concatenate([x_ref[...], b_ref[...], c_ref[...]], axis=1).astype(BF16)
    conv_w = jnp.concatenate([cwx_ref[...], cwb_ref[...], cwc_ref[...]], axis=1)
    conv_b = jnp.concatenate([cbx_ref[...], cbb_ref[...], cbc_ref[...]], axis=1)
    conv = _conv_silu(raw, prev_ref[...], conv_w, conv_b)
    prev_ref[...] = raw[L - SSD_HALO:L]
    x = conv[:, :SSM_GX]
    bm = conv[:, SSM_GX:SSM_GX + SSM_STATE]
    cm = conv[:, SSM_GX + SSM_STATE:]

    dt = jax.nn.softplus(dt_ref[...].astype(F32) + dtb_ref[...])
    lane = lax.broadcasted_iota(jnp.int32, (1, LANES), 1)
    dt = jnp.where(lane < SSM_RANK, dt, 0.0)
    dta = dt * (-jnp.exp(alog_ref[...]) * LOG2E)
    r = lax.broadcasted_iota(jnp.int32, (L, L), 0)
    c = lax.broadcasted_iota(jnp.int32, (L, L), 1)
    tri = r >= c
    tril = jnp.where(tri, 1.0, 0.0).astype(BF16)
    hi, mid, lo = _split3(dta)
    a_col = _dot(tril, hi) + _dot(tril, mid) + _dot(tril, lo)
    a_row = a_col.T
    a_end = a_col[L - 1:L, :]

    stacked = jnp.concatenate([dt, jnp.exp2(a_col), jnp.exp2(a_end - a_col)], axis=0)
    head_of_row = lax.broadcasted_iota(jnp.int32, (LANES, 1), 0)
    head_of_lane = lax.broadcasted_iota(jnp.int32, (1, SSM_GX), 1) // SSM_HEAD_DIM
    spread = jnp.where(head_of_row == head_of_lane, 1.0, 0.0).astype(BF16)
    hi = stacked.astype(BF16)
    lo = (stacked - hi.astype(F32)).astype(BF16)
    wide = _dot(hi, spread) + _dot(lo, spread)
    xdt = x * wide[0:L]
    xdt_b = xdt.astype(BF16)

    cb = jnp.where(tri, _dot_nt(cm.astype(BF16), bm.astype(BF16)), 0.0)
    lane_lo = lax.broadcasted_iota(jnp.int32, (1, LANES), 1) < SSM_HEAD_DIM
    y_parts = []
    for j in range(SSM_RANK // 2):
        xp = xdt_b[:, j * LANES:(j + 1) * LANES]
        ys = []
        for r_ in (2 * j, 2 * j + 1):
            diff = a_col[:, r_:r_ + 1] - a_row[r_:r_ + 1, :]
            mm = cb * jnp.exp2(jnp.minimum(diff, 0.0))
            ys.append(_dot(mm.astype(BF16), xp))
        y_parts.append(jnp.where(lane_lo, ys[0], ys[1]))
    y_in = jnp.concatenate(y_parts, axis=1)

    state = state_ref[...]
    y_st = _dot(cm.astype(BF16), state.astype(BF16)) * wide[L:2 * L]
    xw = (xdt * wide[2 * L:3 * L]).astype(BF16)
    state_ref[...] = state * _expand_heads(jnp.exp2(a_end), 1) + _dot(bm.T.astype(BF16), xw)

    y = y_in + y_st + dexp_ref[...] * x
    o_ref[...] = (y * _silu(z_ref[...].astype(F32))).astype(o_ref.dtype)


def ssd_mixer(proj, conv_w, conv_b, dt_bias, a_log, d_skip, *, batch, seq, offs):
    m = proj.shape[0]
    L = SSD_CHUNK
    nc = seq // L
    G = SSM_GROUPS
    xo, bo, co, dto = (offs[k] for k in ("xs", "B", "C", "dt"))

    def pad_heads(p):
        return jnp.pad(p.reshape(G, 1, SSM_RANK), ((0, 0), (0, 0), (0, LANES - SSM_RANK)))

    d_exp = jnp.repeat(d_skip, SSM_HEAD_DIM).reshape(G, 1, SSM_GX)
    cb2 = conv_b.reshape(1, -1)
    row = lambda b, g, c: b * nc + c
    return pl.pallas_call(
        _ssd_kernel,
        grid=(batch, G, nc),
        in_specs=[pl.BlockSpec((L, SSM_GX), lambda b, g, c: (row(b, g, c), xo // SSM_GX + g)),
                  pl.BlockSpec((L, LANES), lambda b, g, c: (row(b, g, c), bo // LANES + g)),
                  pl.BlockSpec((L, LANES), lambda b, g, c: (row(b, g, c), co // LANES + g)),
                  pl.BlockSpec((L, LANES), lambda b, g, c: (row(b, g, c), dto // LANES + g)),
                  pl.BlockSpec((L, SSM_GX), lambda b, g, c: (row(b, g, c), g)),
                  pl.BlockSpec((CONV_W, SSM_GX), lambda b, g, c: (0, g)),
                  pl.BlockSpec((CONV_W, LANES), lambda b, g, c: (0, W_TOK // LANES + g)),
                  pl.BlockSpec((CONV_W, LANES), lambda b, g, c: (0, (W_TOK + G * SSM_STATE) // LANES + g)),
                  pl.BlockSpec((1, SSM_GX), lambda b, g, c: (0, g)),
                  pl.BlockSpec((1, LANES), lambda b, g, c: (0, W_TOK // LANES + g)),
                  pl.BlockSpec((1, LANES), lambda b, g, c: (0, (W_TOK + G * SSM_STATE) // LANES + g)),
                  pl.BlockSpec((None, 1, LANES), lambda b, g, c: (g, 0, 0)),
                  pl.BlockSpec((None, 1, LANES), lambda b, g, c: (g, 0, 0)),
                  pl.BlockSpec((None, 1, SSM_GX), lambda b, g, c: (g, 0, 0))],
        out_specs=pl.BlockSpec((L, SSM_GX), lambda b, g, c: (row(b, g, c), g)),
        out_shape=jax.ShapeDtypeStruct((m, W_TOK), BF16),
        scratch_shapes=[pltpu.VMEM((SSM_STATE, SSM_GX), F32),
                        pltpu.VMEM((SSD_HALO, SSM_GX + 2 * SSM_STATE), BF16)],
        compiler_params=_cparams(("parallel", "parallel", "arbitrary")),
        name="ssd_mixer",
    )(proj, proj, proj, proj, proj, conv_w, conv_w, conv_w, cb2, cb2, cb2,
      pad_heads(dt_bias), pad_heads(a_log), d_exp)


def _rope_table_kernel(pos_ref, inv_ref, cos_ref, sin_ref):
    ang = pos_ref[...].astype(F32) * inv_ref[...]
    lane = lax.broadcasted_iota(jnp.int32, (1, NSA_HEAD_DIM), 1)
    cos_ref[...] = jnp.cos(ang)
    sin_ref[...] = jnp.where(lane < NSA_HEAD_DIM // 2, -1.0, 1.0) * jnp.sin(ang)


def rope_tables(positions, *, tm):
    m = positions.size
    inv = ROPE_THETA ** (-jnp.arange(0, NSA_HEAD_DIM, 2, dtype=F32) / NSA_HEAD_DIM)
    inv2 = jnp.concatenate([inv, inv]).reshape(1, NSA_HEAD_DIM)
    return pl.pallas_call(
        _rope_table_kernel,
        grid=(m // tm,),
        in_specs=[pl.BlockSpec((tm, 1), lambda i: (i, 0)),
                  pl.BlockSpec((1, NSA_HEAD_DIM), lambda i: (0, 0))],
        out_specs=[pl.BlockSpec((tm, NSA_HEAD_DIM), lambda i: (i, 0))] * 2,
        out_shape=[jax.ShapeDtypeStruct((m, NSA_HEAD_DIM), F32)] * 2,
        compiler_params=_cparams(("parallel",)),
        name="rope_tables",
    )(positions.reshape(m, 1), inv2)


def _nsa_prep_kernel(q_ref, kv_ref, cos_ref, sin_ref, qn_ref, kn_ref,
                     qt_ref, cmp_ref, ks_ref, vst_ref, kw_ref, vwt_ref, *, tm):
    d = NSA_HEAD_DIM
    cos = cos_ref[...]
    sin = sin_ref[...]

    def norm_rope(x, gain, scale):
        x = x.astype(F32)
        x = x * lax.rsqrt(jnp.mean(x * x, axis=-1, keepdims=True) + EPS) * gain
        return (x * cos + pltpu.roll(x, d // 2, 1) * sin) * scale

    for h in range(NSA_KV_HEADS):
        for g in range(NSA_GROUP):
            sl = slice((h * NSA_GROUP + g) * d, (h * NSA_GROUP + g + 1) * d)
            q = norm_rope(q_ref[:, sl], qn_ref[...], d ** -0.5 * LOG2E)
            qt_ref[h, :, g * tm:(g + 1) * tm] = q.T.astype(qt_ref.dtype)
    for h in range(NSA_KV_HEADS):
        ksl = lambda br: slice((2 * br) * NSA_KV_W + h * d, (2 * br) * NSA_KV_W + (h + 1) * d)
        vsl = lambda br: slice((2 * br + 1) * NSA_KV_W + h * d, (2 * br + 1) * NSA_KV_W + (h + 1) * d)
        cmp_ref[:, h * d:(h + 1) * d] = norm_rope(kv_ref[:, ksl(0)], kn_ref[...], 1.0).astype(cmp_ref.dtype)
        cmp_ref[:, NSA_KV_W + h * d:NSA_KV_W + (h + 1) * d] = kv_ref[:, vsl(0)].astype(cmp_ref.dtype)
        for br, k_out, vt_out in ((1, ks_ref, vst_ref), (2, kw_ref, vwt_ref)):
            k_out[:, h * d:(h + 1) * d] = norm_rope(kv_ref[:, ksl(br)], kn_ref[...], 1.0).astype(k_out.dtype)
            vt_out[h] = kv_ref[:, vsl(br)].astype(F32).T.astype(vt_out.dtype)


def nsa_prep(proj, cos, sin, qn, kn, *, offs, tm):
    m = proj.shape[0]
    assert offs["q"] == 0 and offs["kv"] == W_TOK
    d = NSA_HEAD_DIM
    row = lambda w: pl.BlockSpec((tm, w), lambda i: (i, 0))
    vt = pl.BlockSpec((NSA_KV_HEADS, d, tm), lambda i: (0, 0, i))
    return pl.pallas_call(
        functools.partial(_nsa_prep_kernel, tm=tm),
        grid=(m // tm,),
        in_specs=[pl.BlockSpec((tm, W_TOK), lambda i: (i, 0)),
                  pl.BlockSpec((tm, W_TOK), lambda i: (i, 1)),
                  row(d), row(d),
                  pl.BlockSpec((1, d), lambda i: (0, 0)),
                  pl.BlockSpec((1, d), lambda i: (0, 0))],
        out_specs=[pl.BlockSpec((None, NSA_KV_HEADS, d, NSA_GROUP * tm), lambda i: (i, 0, 0, 0)),
                   row(2 * NSA_KV_W), row(NSA_KV_W), vt, row(NSA_KV_W), vt],
        out_shape=[jax.ShapeDtypeStruct((m // tm, NSA_KV_HEADS, d, NSA_GROUP * tm), BF16),
                   jax.ShapeDtypeStruct((m, 2 * NSA_KV_W), BF16),
                   jax.ShapeDtypeStruct((m, NSA_KV_W), BF16),
                   jax.ShapeDtypeStruct((NSA_KV_HEADS, d, m), BF16),
                   jax.ShapeDtypeStruct((m, NSA_KV_W), BF16),
                   jax.ShapeDtypeStruct((NSA_KV_HEADS, d, m), BF16)],
        compiler_params=_cparams(("parallel",)),
        name="nsa_prep",
    )(proj, proj, cos, sin, qn.reshape(1, -1), kn.reshape(1, -1))


def _nsa_compress_kernel(r_ref, pos_ref, kw1_ref, kw2_ref, vw1_ref, vw2_ref, kc_ref, vct_ref):
    d = NSA_HEAD_DIM
    half = CMP_STRIDE * d
    n = r_ref.shape[0]
    posb = jnp.broadcast_to(pos_ref[...], (8, CMP_BLOCK * d)).astype(BF16)
    for off, w1_ref, w2_ref, out, transposed in ((0, kw1_ref, kw2_ref, kc_ref, False),
                                                 (NSA_KV_W, vw1_ref, vw2_ref, vct_ref, True)):
        pos_term = _dot(posb, w1_ref[...])[0:1]
        for h in range(NSA_KV_HEADS):
            cat = jnp.concatenate(
                [r_ref[:, tt * 2 * NSA_KV_W + off + h * d: tt * 2 * NSA_KV_W + off + (h + 1) * d]
                 for tt in range(CMP_STRIDE)], axis=1)
            ha = _dot(cat, w1_ref[0:half, :])
            hb = _dot(cat, w1_ref[half:2 * half, :])
            hsum = ha + pltpu.roll(hb, n - 1, 0) + pos_term
            res = _dot(_silu(hsum).astype(BF16), w2_ref[...])
            out[h] = (res.T if transposed else res).astype(out.dtype)


def nsa_compress(cmp_in, cmp_pos, kw1, kw2, vw1, vw2, *, batch, seq):
    n = seq // CMP_STRIDE
    wide = CMP_STRIDE * 2 * NSA_KV_W
    d = NSA_HEAD_DIM
    r = cmp_in.reshape(batch * n, wide)
    full = lambda shape: pl.BlockSpec(shape, lambda b: (0,) * len(shape))
    return pl.pallas_call(
        _nsa_compress_kernel,
        grid=(batch,),
        in_specs=[pl.BlockSpec((n, wide), lambda b: (b, 0)),
                  full((1, CMP_BLOCK * d)), full((CMP_BLOCK * d, d)), full((d, d)),
                  full((CMP_BLOCK * d, d)), full((d, d))],
        out_specs=[pl.BlockSpec((None, NSA_KV_HEADS, n, d), lambda b: (b, 0, 0, 0)),
                   pl.BlockSpec((None, NSA_KV_HEADS, d, n), lambda b: (b, 0, 0, 0))],
        out_shape=[jax.ShapeDtypeStruct((batch, NSA_KV_HEADS, n, d), BF16),
                   jax.ShapeDtypeStruct((batch, NSA_KV_HEADS, d, n), BF16)],
        compiler_params=_cparams(("parallel",)),
        name="nsa_compress",
    )(r, cmp_pos.reshape(1, CMP_BLOCK * d), kw1.astype(BF16), kw2.astype(BF16), vw1.astype(BF16), vw2.astype(BF16))


def _store_gated_t(o_t, g, gate_t, o_ref, branch, others=(), out_gate_ref=None):
    lane = branch * NSA_GROUP + g
    sl = slice(g * NSA_HEAD_DIM, (g + 1) * NSA_HEAD_DIM)
    gate = jax.nn.sigmoid(gate_t[lane:lane + 1, :])
    o = (o_t * gate).T
    for ref in others:
        o = o + ref[:, sl].astype(F32)
    if out_gate_ref is not None:
        o = o * _silu(out_gate_ref[:, sl].astype(F32))
    o_ref[:, sl] = o.astype(o_ref.dtype)


def _nsa_cmp_kernel(qt_ref, kc_ref, vct_ref, gl_ref, o_ref, selt_ref, *, tq, n_sel):
    i = pl.program_id(2)
    nc = kc_ref.shape[0]
    t = i * tq + lax.broadcasted_iota(jnp.int32, (1, tq), 1)
    cend = lax.broadcasted_iota(jnp.int32, (nc, 1), 0) * CMP_STRIDE + (CMP_BLOCK - 1)
    cmask = cend <= t
    any_vis = jnp.where(t >= CMP_BLOCK - 1, 1.0, 0.0)
    gate_t = gl_ref[...].astype(F32).T
    s_all = _dot(kc_ref[...], qt_ref[...])
    vct = vct_ref[...]
    psum = jnp.zeros((nc, tq), F32)
    for g in range(NSA_GROUP):
        s = jnp.where(cmask, s_all[:, g * tq:(g + 1) * tq], NEG_INF)
        e = jnp.exp2(s - jnp.max(s, axis=0, keepdims=True))
        p = e * (any_vis / jnp.sum(e, axis=0, keepdims=True))
        psum = psum + p
        _store_gated_t(_dot(vct, p.astype(BF16)), g, gate_t, o_ref, 0)

    js = lax.broadcasted_iota(jnp.int32, (n_sel, nc), 0) * SEL_BLOCK
    cs = lax.broadcasted_iota(jnp.int32, (n_sel, nc), 1) * CMP_STRIDE
    agg_t = jnp.where((cs < js + SEL_BLOCK) & (cs + CMP_BLOCK - 1 >= js), 1.0, 0.0).astype(BF16)
    hi, mid, lo = _split3(psum)
    imp = _dot(agg_t, hi) + _dot(agg_t, mid) + _dot(agg_t, lo)
    j = lax.broadcasted_iota(jnp.int32, (n_sel, 1), 0)
    bt = t // SEL_BLOCK
    forced = (j == 0) | (j == bt) | (j == bt - 1)
    sc = jnp.where(forced, SEL_BIG, jnp.where(j <= bt, imp, -SEL_BIG))
    rowi = lax.broadcasted_iota(jnp.int32, (n_sel, tq), 0).astype(F32)
    sel = jnp.zeros((n_sel, tq), F32)
    for _ in range(min(SEL_TOPK, n_sel)):
        mx = jnp.max(sc, axis=0, keepdims=True)
        idx = jnp.min(jnp.where(sc == mx, rowi, float(n_sel)), axis=0, keepdims=True)
        pick = rowi == idx
        sel = jnp.where(pick, 1.0, sel)
        sc = jnp.where(pick, -jnp.inf, sc)
    selt_ref[...] = sel


def nsa_cmp_attn(q_t, kc, vct, proj, *, batch, seq, gl_off, tq):
    m = batch * seq
    nq = seq // tq
    n_cmp = kc.shape[2]
    n_sel = seq // SEL_BLOCK
    d = NSA_HEAD_DIM
    gw = NSA_GROUP * d
    glb = gl_off // LANES
    return pl.pallas_call(
        functools.partial(_nsa_cmp_kernel, tq=tq, n_sel=n_sel),
        grid=(batch, NSA_KV_HEADS, nq),
        in_specs=[pl.BlockSpec((None, None, d, NSA_GROUP * tq), lambda b, h, i: (b * nq + i, h, 0, 0)),
                  pl.BlockSpec((None, None, n_cmp, d), lambda b, h, i: (b, h, 0, 0)),
                  pl.BlockSpec((None, None, d, n_cmp), lambda b, h, i: (b, h, 0, 0)),
                  pl.BlockSpec((tq, LANES), lambda b, h, i: (b * nq + i, glb + h))],
        out_specs=[pl.BlockSpec((tq, gw), lambda b, h, i: (b * nq + i, h)),
                   pl.BlockSpec((None, None, n_sel, tq), lambda b, h, i: (b, h, 0, i))],
        out_shape=[jax.ShapeDtypeStruct((m, W_TOK), BF16),
                   jax.ShapeDtypeStruct((batch, NSA_KV_HEADS, n_sel, seq), F32)],
        compiler_params=_cparams(("parallel", "parallel", "parallel")),
        name="nsa_cmp_attn",
    )(q_t, kc, vct, proj)


def _nsa_flash_kernel(qi_ref, ki_ref, first_ref, last_ref, qt_ref, k_ref, vt_ref, *rest, tq, tk, branch):
    if branch == 1:
        selt_ref, gl_ref, o_ref, m_ref, l_ref, acc_ref = rest
        others, out_gate_ref = (), None
    else:
        gl_ref, ocmp_ref, osel_ref, out_gate_ref, o_ref, m_ref, l_ref, acc_ref = rest
        others = (ocmp_ref, osel_ref)
    step = pl.program_id(2)
    qi = qi_ref[step]
    ki = ki_ref[step]

    @pl.when(first_ref[step] == 1)
    def _():
        m_ref[...] = jnp.full_like(m_ref, NEG_INF)
        l_ref[...] = jnp.zeros_like(l_ref)
        acc_ref[...] = jnp.zeros_like(acc_ref)

    t = qi * tq + lax.broadcasted_iota(jnp.int32, (1, tq), 1)
    kp = ki * tk + lax.broadcasted_iota(jnp.int32, (tk, 1), 0)
    if branch == 1:
        per_tile = tk // SEL_BLOCK
        per_load = 8 // per_tile
        rows8 = selt_ref[pl.ds(pl.multiple_of((ki // per_load) * 8, 8), 8), :]
        sub = ki % per_load
        rows = rows8[0:per_tile]
        for u in range(1, per_load):
            rows = jnp.where(sub == u, rows8[u * per_tile:(u + 1) * per_tile], rows)
        selm = jnp.concatenate([jnp.broadcast_to(rows[u:u + 1], (SEL_BLOCK, tq)) for u in range(per_tile)], axis=0)
        mask = (selm > 0.5) & (kp <= t)
    else:
        mask = (kp <= t) & (kp > t - WINDOW)
    bias = jnp.where(mask, 0.0, -jnp.inf)
    s_all = _dot(k_ref[...], qt_ref[...])
    vt = vt_ref[...]
    m_all = m_ref[...]
    l_all = l_ref[...]
    m_rows, l_rows = [], []
    for g in range(NSA_GROUP):
        cols = slice(g * tq, (g + 1) * tq)
        s = s_all[:, cols] + bias
        m_old = m_all[g:g + 1, :]
        m_new = jnp.maximum(m_old, jnp.max(s, axis=0, keepdims=True))
        alpha = jnp.exp2(m_old - m_new)
        p = jnp.exp2(s - m_new)
        l_rows.append(alpha * l_all[g:g + 1, :] + jnp.sum(p, axis=0, keepdims=True))
        m_rows.append(m_new)
        acc_ref[:, cols] = alpha * acc_ref[:, cols] + _dot(vt, p.astype(BF16))
    m_ref[...] = jnp.concatenate(m_rows, axis=0)
    l_ref[...] = jnp.concatenate(l_rows, axis=0)

    @pl.when(last_ref[step] == 1)
    def _():
        gate_t = gl_ref[...].astype(F32).T
        for g in range(NSA_GROUP):
            o_t = acc_ref[:, g * tq:(g + 1) * tq] / l_ref[g:g + 1, :]
            _store_gated_t(o_t, g, gate_t, o_ref, branch, others, out_gate_ref)


def nsa_flash_attn(q_t, k, v_t, proj, *, batch, seq, gl_off, branch, tq, tk, sel_t=None, others=(), gate_off=None):
    m = batch * seq
    nq = seq // tq
    nk = seq // tk
    d = NSA_HEAD_DIM
    gw = NSA_GROUP * d
    glb = gl_off // LANES
    n_sel = seq // SEL_BLOCK
    assert 8 % (tk // SEL_BLOCK) == 0 and n_sel % 8 == 0
    steps = []
    for i in range(nq):
        hi_k = ((i + 1) * tq - 1) // tk
        lo_k = 0 if branch == 1 else max(0, (i * tq - WINDOW + 1) // tk)
        for kk in range(lo_k, hi_k + 1):
            steps.append((i, kk, int(kk == lo_k), int(kk == hi_k)))
    tabs = [jnp.asarray(np.array([s_[c] for s_ in steps], np.int32)) for c in range(4)]
    q_tile = lambda col: pl.BlockSpec((tq, gw), lambda b, h, s, qi, ki, fi, la: (b * nq + qi[s], col + h))
    gl_spec = pl.BlockSpec((tq, LANES), lambda b, h, s, qi, ki, fi, la: (b * nq + qi[s], glb + h))
    in_specs = [pl.BlockSpec((None, None, d, NSA_GROUP * tq), lambda b, h, s, qi, ki, fi, la: (b * nq + qi[s], h, 0, 0)),
                pl.BlockSpec((tk, d), lambda b, h, s, qi, ki, fi, la: (b * nk + ki[s], h)),
                pl.BlockSpec((None, d, tk), lambda b, h, s, qi, ki, fi, la: (h, 0, b * nk + ki[s]))]
    if branch == 1:
        in_specs += [pl.BlockSpec((None, None, n_sel, tq), lambda b, h, s, qi, ki, fi, la: (b, h, 0, qi[s])), gl_spec]
        operands = (sel_t, proj)
    else:
        assert gate_off % gw == 0
        in_specs += [gl_spec, q_tile(0), q_tile(0), q_tile(gate_off // gw)]
        operands = (proj, *others, proj)
    grid_spec = pltpu.PrefetchScalarGridSpec(
        num_scalar_prefetch=4,
        grid=(batch, NSA_KV_HEADS, len(steps)),
        in_specs=in_specs,
        out_specs=q_tile(0),
        scratch_shapes=[pltpu.VMEM((NSA_GROUP, tq), F32),
                        pltpu.VMEM((NSA_GROUP, tq), F32),
                        pltpu.VMEM((d, NSA_GROUP * tq), F32)])
    return pl.pallas_call(
        functools.partial(_nsa_flash_kernel, tq=tq, tk=tk, branch=branch),
        grid_spec=grid_spec,
        out_shape=jax.ShapeDtypeStruct((m, W_TOK), BF16),
        compiler_params=_cparams(("parallel", "parallel", "arbitrary")),
        name="nsa_sel_attn" if branch == 1 else "nsa_win_attn",
    )(*tabs, q_t, k, v_t, *operands)


def _nsa_layout(w_in):
    kv_end = W_TOK + 6 * NSA_KV_W
    gl_end = kv_end + 3 * NSA_HEADS
    k = w_in.shape[0]
    glw = w_in[:, kv_end:gl_end].reshape(k, NSA_KV_HEADS, NSA_GROUP, 3)
    glw = glw.transpose(0, 1, 3, 2).reshape(k, NSA_KV_HEADS, 3 * NSA_GROUP)
    glw = jnp.pad(glw, ((0, 0), (0, 0), (0, LANES - 3 * NSA_GROUP))).reshape(k, NSA_KV_HEADS * LANES)
    pad_w = 2 * W_TOK - kv_end - NSA_KV_HEADS * LANES
    w = jnp.concatenate([w_in[:, :kv_end], glw, jnp.zeros((k, pad_w), w_in.dtype), w_in[:, gl_end:]], axis=1)
    offs = {"q": 0, "kv": W_TOK, "gl": kv_end, "gate": 2 * W_TOK, "memq": 3 * W_TOK}
    return w.astype(BF16), offs


def _ssd_layout(w_in):
    k = w_in.shape[0]
    xbc_end = W_TOK + W_TOK + 2 * SSM_GROUPS * SSM_STATE
    dt_end = xbc_end + SSM_HEADS
    dtw = w_in[:, xbc_end:dt_end].reshape(k, SSM_GROUPS, SSM_RANK)
    dtw = jnp.pad(dtw, ((0, 0), (0, 0), (0, LANES - SSM_RANK))).reshape(k, SSM_GROUPS * LANES)
    w = jnp.concatenate([w_in[:, :xbc_end], w_in[:, dt_end:], dtw], axis=1)
    offs = {"z": 0, "xs": W_TOK, "B": 2 * W_TOK, "C": 2 * W_TOK + SSM_GROUPS * SSM_STATE,
            "memq": xbc_end, "dt": xbc_end + 2 * W_MEM}
    return w.astype(BF16), offs


def nsa_mixer(proj, positions, qn, kn, cmp_pos, kw1, kw2, vw1, vw2, *, batch, seq, offs):
    cos, sin = rope_tables(positions, tm=min(1024, batch * seq))
    tq = 256
    common = dict(batch=batch, seq=seq, gl_off=offs["gl"])
    q_t, cmp_in, k_s, vt_s, k_w, vt_w = nsa_prep(proj, cos, sin, qn, kn, offs=offs, tm=tq)
    kc, vct = nsa_compress(cmp_in, cmp_pos, kw1, kw2, vw1, vw2, batch=batch, seq=seq)
    o_cmp, sel_t = nsa_cmp_attn(q_t, kc, vct, proj, tq=tq, **common)
    o_sel = nsa_flash_attn(q_t, k_s, vt_s, proj, branch=1, tq=tq, tk=256, sel_t=sel_t, **common)
    return nsa_flash_attn(q_t, k_w, vt_w, proj, branch=2, tq=tq, tk=256, others=(o_cmp, o_sel),
                          gate_off=offs["gate"], **common)


def _layer(x2, mem2, mem_norm, norm, w_in_b, w_out, mem_wkv, mem_qn, mem_kn, mixer, memq_off, *, seq, gnorm=None):
    m = x2.shape[0]
    proj = norm_matmul(x2, norm, w_in_b, tm=_tile(m, 1024), tn=_tile(w_in_b.shape[1], 1024), out_dtype=BF16)
    kv = norm_matmul(mem2, mem_norm, mem_wkv.astype(BF16), tm=_tile(mem2.shape[0], 512), tn=512, out_dtype=F32)
    y_mem = mem_attn(proj, kv, mem_qn, mem_kn, q_off=memq_off, seq=seq, tq=min(512, seq))
    y_tok = mixer(proj)
    w_tok = w_out[:W_TOK].astype(BF16)
    w_mem = w_out[W_TOK:].astype(BF16)
    norm_tok = gnorm is not None
    gain = gnorm if norm_tok else jnp.ones((W_TOK,), F32)
    tm, tn = (_tile(m, 512), 1024) if norm_tok else (_tile(m, 1024), 512)
    return out_proj(y_tok, y_mem, w_tok, w_mem, x2, gain, norm_tok=norm_tok, tm=tm, tn=tn)


def kernel(x, mem, positions, mem_norm, l0_norm, l0_w_in, l0_w_out, l0_mem_wkv, l0_mem_qnorm, l0_mem_knorm, l0_pool_w, l0_pool_scale, l1_norm, l1_w_in, l1_w_out, l1_mem_wkv, l1_mem_qnorm, l1_mem_knorm, l1_qnorm, l1_knorm, l1_cmp_pos, l1_cmp_k_w1, l1_cmp_k_w2, l1_cmp_v_w1, l1_cmp_v_w2, l2_norm, l2_w_in, l2_w_out, l2_mem_wkv, l2_mem_qnorm, l2_mem_knorm, l2_conv_w, l2_conv_b, l2_dt_bias, l2_A_log, l2_D, l2_gnorm, l3_norm, l3_w_in, l3_w_out, l3_mem_wkv, l3_mem_qnorm, l3_mem_knorm, l3_v_norm, l3_sgu_w, l3_sgu_b):
    batch, seq, d = x.shape
    x2 = x.reshape(batch * seq, d)
    mem2 = mem.reshape(batch * MEM_TOKENS, d)

    x2 = _layer(x2, mem2, mem_norm, l0_norm, l0_w_in.astype(BF16), l0_w_out, l0_mem_wkv, l0_mem_qnorm, l0_mem_knorm,
                lambda p: pool_mixer(p, l0_pool_w, l0_pool_scale, seq=seq, tm=256), 2 * W_TOK, seq=seq)

    w1, offs1 = _nsa_layout(l1_w_in)
    x2 = _layer(x2, mem2, mem_norm, l1_norm, w1, l1_w_out, l1_mem_wkv, l1_mem_qnorm, l1_mem_knorm,
                lambda p: nsa_mixer(p, positions, l1_qnorm, l1_knorm, l1_cmp_pos, l1_cmp_k_w1, l1_cmp_k_w2,
                                    l1_cmp_v_w1, l1_cmp_v_w2, batch=batch, seq=seq, offs=offs1),
                offs1["memq"], seq=seq)

    w2, offs2 = _ssd_layout(l2_w_in)
    x2 = _layer(x2, mem2, mem_norm, l2_norm, w2, l2_w_out, l2_mem_wkv, l2_mem_qnorm, l2_mem_knorm,
                lambda p: ssd_mixer(p, l2_conv_w, l2_conv_b, l2_dt_bias, l2_A_log, l2_D, batch=batch, seq=seq, offs=offs2),
                offs2["memq"], seq=seq, gnorm=l2_gnorm)

    x2 = _layer(x2, mem2, mem_norm, l3_norm, l3_w_in.astype(BF16), l3_w_out, l3_mem_wkv, l3_mem_qnorm, l3_mem_knorm,
                lambda p: sgu_mixer(p, l3_v_norm, l3_sgu_w, l3_sgu_b, tm=256), 3 * W_TOK, seq=seq)
    return x2.reshape(batch, seq, d)
```

```python
import functools
import math

import numpy as np
import jax
import jax.numpy as jnp
from jax import lax
from jax.experimental import pallas as pl
from jax.experimental.pallas import tpu as pltpu

F32 = jnp.float32
BF16 = jnp.bfloat16

D_MODEL = 2048
EPS = 1e-6
ROPE_THETA = 10000.0
NEG_INF = -1e30
SEL_BIG = 1e9
LOG2E = 1.0 / math.log(2.0)

W_TOK = 2 * D_MODEL
MEM_TOKENS = 256
MEM_HEADS = 4
MEM_HEAD_DIM = D_MODEL // 8
W_MEM = MEM_HEADS * MEM_HEAD_DIM

POOL_GROUPS = 4
POOL_GW = W_TOK // POOL_GROUPS
POOL_HALO = 16

NSA_HEADS = 32
NSA_KV_HEADS = 4
NSA_GROUP = NSA_HEADS // NSA_KV_HEADS
NSA_HEAD_DIM = 128
NSA_KV_W = NSA_KV_HEADS * NSA_HEAD_DIM
CMP_BLOCK = 32
CMP_STRIDE = 16
SEL_BLOCK = 64
SEL_TOPK = 16
WINDOW = 512
ROWSUM_ROWS = 16

SSM_HEAD_DIM = 64
SSM_HEADS = W_TOK // SSM_HEAD_DIM
SSM_GROUPS = 8
SSM_RANK = SSM_HEADS // SSM_GROUPS
SSM_STATE = 128
CONV_W = 4
SSD_CHUNK = 256
SSM_GX = W_TOK // SSM_GROUPS
SSD_HALO = 16

SGU_CHUNK = 128
SGU_GROUPS = 8
SGU_GW = W_TOK // SGU_GROUPS

LANES = 128
VMEM_LIMIT = 56 * 1024 * 1024


def _cparams(sem):
    return pltpu.CompilerParams(dimension_semantics=sem, vmem_limit_bytes=VMEM_LIMIT)


def _tile(n, pref):
    t = pref
    while n % t:
        t //= 2
    return t


def _silu(x):
    return x * jax.nn.sigmoid(x)


def _gelu_exact(x):
    return 0.5 * x * (1.0 + lax.erf(x * (1.0 / math.sqrt(2.0))))


def _dot(a, b):
    return jnp.dot(a, b, preferred_element_type=F32)


def _dot_nt(a, b):
    return lax.dot_general(a, b, (((1,), (1,)), ((), ())), preferred_element_type=F32)


def _split3(x):
    hi = x.astype(BF16)
    r1 = x - hi.astype(F32)
    mid = r1.astype(BF16)
    lo = (r1 - mid.astype(F32)).astype(BF16)
    return hi, mid, lo


def _norm_matmul_kernel(x_ref, g_ref, w_ref, o_ref, h_ref):
    @pl.when(pl.program_id(1) == 0)
    def _():
        x = x_ref[...].astype(F32)
        ms = jnp.mean(x * x, axis=-1, keepdims=True)
        h_ref[...] = (x * lax.rsqrt(ms + EPS) * g_ref[...]).astype(BF16)

    o_ref[...] = _dot(h_ref[...], w_ref[...]).astype(o_ref.dtype)


def norm_matmul(x, g, w, *, tm, tn, out_dtype):
    m, k = x.shape
    n = w.shape[1]
    return pl.pallas_call(
        _norm_matmul_kernel,
        grid=(m // tm, n // tn),
        in_specs=[pl.BlockSpec((tm, k), lambda i, j: (i, 0)),
                  pl.BlockSpec((1, k), lambda i, j: (0, 0)),
                  pl.BlockSpec((k, tn), lambda i, j: (0, j))],
        out_specs=pl.BlockSpec((tm, tn), lambda i, j: (i, j)),
        out_shape=jax.ShapeDtypeStruct((m, n), out_dtype),
        scratch_shapes=[pltpu.VMEM((tm, k), BF16)],
        compiler_params=_cparams(("parallel", "arbitrary")),
        name="norm_matmul",
    )(x, g.reshape(1, k), w)


def _out_proj_kernel(yt_ref, ym_ref, wt_ref, wm_ref, x_ref, g_ref, o_ref, *scratch, norm_tok):
    if norm_tok:
        (h_ref,) = scratch

        @pl.when(pl.program_id(1) == 0)
        def _():
            y = yt_ref[...].astype(F32)
            ms = jnp.mean(y * y, axis=-1, keepdims=True)
            h_ref[...] = (y * lax.rsqrt(ms + EPS) * g_ref[...]).astype(BF16)

        yt = h_ref[...]
    else:
        yt = yt_ref[...]
    acc = _dot(yt, wt_ref[...]) + _dot(ym_ref[...], wm_ref[...])
    o_ref[...] = x_ref[...] + acc


def out_proj(y_tok, y_mem, w_tok, w_mem, x, gain, *, norm_tok, tm, tn):
    m, kt = y_tok.shape
    km = y_mem.shape[1]
    n = w_tok.shape[1]
    return pl.pallas_call(
        functools.partial(_out_proj_kernel, norm_tok=norm_tok),
        grid=(m // tm, n // tn),
        in_specs=[pl.BlockSpec((tm, kt), lambda i, j: (i, 0)),
                  pl.BlockSpec((tm, km), lambda i, j: (i, 0)),
                  pl.BlockSpec((kt, tn), lambda i, j: (0, j)),
                  pl.BlockSpec((km, tn), lambda i, j: (0, j)),
                  pl.BlockSpec((tm, tn), lambda i, j: (i, j)),
                  pl.BlockSpec((1, kt), lambda i, j: (0, 0))],
        out_specs=pl.BlockSpec((tm, tn), lambda i, j: (i, j)),
        out_shape=jax.ShapeDtypeStruct((m, n), F32),
        scratch_shapes=[pltpu.VMEM((tm, kt), BF16)] if norm_tok else [],
        compiler_params=_cparams(("parallel", "arbitrary")),
        name="out_proj",
    )(y_tok, y_mem, w_tok, w_mem, x, gain.reshape(1, kt))


def _mem_attn_kernel(q_ref, gate_ref, kv_ref, qn_ref, kn_ref, o_ref):
    hd = MEM_HEAD_DIM
    scale = hd ** -0.5
    for h in range(MEM_HEADS):
        q = q_ref[:, h * hd:(h + 1) * hd].astype(F32)
        q = q * lax.rsqrt(jnp.mean(q * q, axis=-1, keepdims=True) + EPS) * qn_ref[...]
        k = kv_ref[:, h * hd:(h + 1) * hd].astype(F32)
        k = k * lax.rsqrt(jnp.mean(k * k, axis=-1, keepdims=True) + EPS) * kn_ref[...]
        v = kv_ref[:, W_MEM + h * hd:W_MEM + (h + 1) * hd].astype(BF16)
        s = _dot_nt((q * scale).astype(BF16), k.astype(BF16))
        e = jnp.exp(s - jnp.max(s, axis=-1, keepdims=True))
        o = _dot(e.astype(BF16), v) / jnp.sum(e, axis=-1, keepdims=True)
        gate = gate_ref[:, h * hd:(h + 1) * hd].astype(F32)
        o_ref[:, h * hd:(h + 1) * hd] = (o * _silu(gate)).astype(o_ref.dtype)


def mem_attn(proj, kv, qn, kn, *, q_off, seq, tq):
    m = proj.shape[0]
    nq = seq // tq
    qb = q_off // W_MEM
    return pl.pallas_call(
        _mem_attn_kernel,
        grid=(m // tq,),
        in_specs=[pl.BlockSpec((tq, W_MEM), lambda i: (i, qb)),
                  pl.BlockSpec((tq, W_MEM), lambda i: (i, qb + 1)),
                  pl.BlockSpec((MEM_TOKENS, 2 * W_MEM), lambda i: (i // nq, 0)),
                  pl.BlockSpec((1, MEM_HEAD_DIM), lambda i: (0, 0)),
                  pl.BlockSpec((1, MEM_HEAD_DIM), lambda i: (0, 0))],
        out_specs=pl.BlockSpec((tq, W_MEM), lambda i: (i, 0)),
        out_shape=jax.ShapeDtypeStruct((m, W_MEM), BF16),
        compiler_params=_cparams(("parallel",)),
        name="mem_attn",
    )(proj, proj, kv, qn.reshape(1, -1), kn.reshape(1, -1))


def _pool_kernel(v_ref, halo_ref, gate_ref, w_ref, scale_ref, o_ref, *, tm, seq):
    g = pl.program_id(0)
    i = pl.program_id(1)
    win = jnp.left_shift(2, g)
    t_seq = lax.rem(i * tm, seq)
    r = lax.broadcasted_iota(jnp.int32, (tm, 1), 0)
    c = lax.broadcasted_iota(jnp.int32, (1, tm), 1)
    a_main = jnp.where((c <= r) & (c > r - win), 1.0, 0.0).astype(BF16)
    ch = lax.broadcasted_iota(jnp.int32, (1, POOL_HALO), 1)
    a_halo = jnp.where((ch > r + POOL_HALO - win) & (t_seq > 0), 1.0, 0.0).astype(BF16)
    v = v_ref[...]
    wsum = _dot(a_main, v.astype(BF16)) + _dot(a_halo, halo_ref[...].astype(BF16))
    cnt = jnp.minimum(t_seq + r + 1, win).astype(F32)
    mix = wsum / cnt - v.astype(F32)
    out = _dot(mix.astype(BF16), w_ref[...]) * scale_ref[...]
    o_ref[...] = (out * _silu(gate_ref[...].astype(F32))).astype(o_ref.dtype)


def pool_mixer(proj, pool_w, pool_scale, *, seq, tm):
    m = proj.shape[0]
    hb = tm // POOL_HALO
    return pl.pallas_call(
        functools.partial(_pool_kernel, tm=tm, seq=seq),
        grid=(POOL_GROUPS, m // tm),
        in_specs=[pl.BlockSpec((tm, POOL_GW), lambda g, i: (i, g)),
                  pl.BlockSpec((POOL_HALO, POOL_GW), lambda g, i: (jnp.maximum(i * hb - 1, 0), g)),
                  pl.BlockSpec((tm, POOL_GW), lambda g, i: (i, POOL_GROUPS + g)),
                  pl.BlockSpec((None, POOL_GW, POOL_GW), lambda g, i: (g, 0, 0)),
                  pl.BlockSpec((1, POOL_GW), lambda g, i: (0, g))],
        out_specs=pl.BlockSpec((tm, POOL_GW), lambda g, i: (i, g)),
        out_shape=jax.ShapeDtypeStruct((m, W_TOK), BF16),
        compiler_params=_cparams(("parallel", "parallel")),
        name="pool_mixer",
    )(proj, proj, proj, pool_w.astype(BF16), pool_scale.reshape(1, W_TOK))


def _sgu_kernel(u_ref, v_ref, gate_ref, vn_ref, w_ref, bt_ref, o_ref, vs_ref, *, tm):
    ssq = jnp.zeros((tm, 1), F32)
    for g in range(SGU_GROUPS):
        sl = slice(g * SGU_GW, (g + 1) * SGU_GW)
        vg = _gelu_exact(v_ref[:, sl].astype(F32))
        vs_ref[:, sl] = vg
        ssq = ssq + jnp.sum(vg * vg, axis=-1, keepdims=True)
    inv = lax.rsqrt(ssq * (1.0 / W_TOK) + EPS)
    r = lax.broadcasted_iota(jnp.int32, (SGU_CHUNK, SGU_CHUNK), 0)
    c = lax.broadcasted_iota(jnp.int32, (SGU_CHUNK, SGU_CHUNK), 1)
    for g in range(SGU_GROUPS):
        sl = slice(g * SGU_GW, (g + 1) * SGU_GW)
        w = jnp.where(r >= c, w_ref[g], 0.0).astype(BF16)
        vn = (vs_ref[:, sl] * inv * vn_ref[:, sl]).astype(BF16)
        for ck in range(tm // SGU_CHUNK):
            rs = slice(ck * SGU_CHUNK, (ck + 1) * SGU_CHUNK)
            mixed = _dot(w, vn[rs]) + bt_ref[:, g:g + 1]
            u = _gelu_exact(u_ref[rs, sl].astype(F32))
            o_ref[rs, sl] = (u * mixed * _silu(gate_ref[rs, sl].astype(F32))).astype(o_ref.dtype)


def sgu_mixer(proj, v_norm, sgu_w, sgu_b, *, tm):
    m = proj.shape[0]
    return pl.pallas_call(
        functools.partial(_sgu_kernel, tm=tm),
        grid=(m // tm,),
        in_specs=[pl.BlockSpec((tm, W_TOK), lambda i: (i, 0)),
                  pl.BlockSpec((tm, W_TOK), lambda i: (i, 1)),
                  pl.BlockSpec((tm, W_TOK), lambda i: (i, 2)),
                  pl.BlockSpec((1, W_TOK), lambda i: (0, 0)),
                  pl.BlockSpec((SGU_GROUPS, SGU_CHUNK, SGU_CHUNK), lambda i: (0, 0, 0)),
                  pl.BlockSpec((SGU_CHUNK, SGU_GROUPS), lambda i: (0, 0))],
        out_specs=pl.BlockSpec((tm, W_TOK), lambda i: (i, 0)),
        out_shape=jax.ShapeDtypeStruct((m, W_TOK), BF16),
        scratch_shapes=[pltpu.VMEM((tm, W_TOK), F32)],
        compiler_params=_cparams(("parallel",)),
        name="sgu_mixer",
    )(proj, proj, proj, v_norm.reshape(1, W_TOK), sgu_w, sgu_b.T)


def _conv_silu(raw, prev, w, b):
    L = raw.shape[0]
    rr = lax.broadcasted_iota(jnp.int32, ((CONV_W - 1) * L, 1), 0)
    src = rr % L - (rr // L + 1)
    main = jnp.where(lax.broadcasted_iota(jnp.int32, (1, L), 1) == src, 1.0, 0.0).astype(BF16)
    shifted = _dot(main, raw)
    row8 = lax.broadcasted_iota(jnp.int32, (8, 1), 0)
    prev32 = prev.astype(F32)
    acc = raw.astype(F32) * w[CONV_W - 1:CONV_W, :] + b
    for k in range(1, CONV_W):
        blk = shifted[(k - 1) * L:k * L]
        head = blk[0:8] + jnp.where(row8 < k, pltpu.roll(prev32, k, 0)[0:8], 0.0)
        blk = jnp.concatenate([head, blk[8:]], axis=0)
        acc = acc + blk * w[CONV_W - 1 - k:CONV_W - k, :]
    return _silu(acc)


def _expand_heads(cols, n_rows):
    lane_head = lax.broadcasted_iota(jnp.int32, (1, SSM_GX), 1) // SSM_HEAD_DIM
    out = jnp.zeros((n_rows, SSM_GX), F32)
    for r in range(SSM_RANK):
        out = jnp.where(lane_head == r, cols[:, r:r + 1], out)
    return out


def _ssd_kernel(x_ref, b_ref, c_ref, dt_ref, z_ref, cwx_ref, cwb_ref, cwc_ref, cbx_ref, cbb_ref, cbc_ref,
                dtb_ref, alog_ref, dexp_ref, o_ref, state_ref, prev_ref):
    L = SSD_CHUNK
    ck = pl.program_id(2)

    @pl.when(ck == 0)
    def _():
        state_ref[...] = jnp.zeros_like(state_ref)
        prev_ref[...] = jnp.zeros_like(prev_ref)

    raw = jnp.concatenate([x_ref[...], b_ref[...], c_ref[...]], axis=1).astype(BF16)
    conv_w = jnp.concatenate([cwx_ref[...], cwb_ref[...], cwc_ref[...]], axis=1)
    conv_b = jnp.concatenate([cbx_ref[...], cbb_ref[...], cbc_ref[...]], axis=1)
    conv = _conv_silu(raw, prev_ref[...], conv_w, conv_b)
    prev_ref[...] = raw[L - SSD_HALO:L]
    x = conv[:, :SSM_GX]
    bm = conv[:, SSM_GX:SSM_GX + SSM_STATE]
    cm = conv[:, SSM_GX + SSM_STATE:]

    dt = jax.nn.softplus(dt_ref[...].astype(F32) + dtb_ref[...])
    lane = lax.broadcasted_iota(jnp.int32, (1, LANES), 1)
    dt = jnp.where(lane < SSM_RANK, dt, 0.0)
    dta = dt * (-jnp.exp(alog_ref[...]) * LOG2E)
    r = lax.broadcasted_iota(jnp.int32, (L, L), 0)
    c = lax.broadcasted_iota(jnp.int32, (L, L), 1)
    tri = r >= c
    tril = jnp.where(tri, 1.0, 0.0).astype(BF16)
    hi, mid, lo = _split3(dta)
    a_col = _dot(tril, hi) + _dot(tril, mid) + _dot(tril, lo)
    a_row = a_col.T
    a_end = a_col[L - 1:L, :]

    stacked = jnp.concatenate([dt, jnp.exp2(a_col), jnp.exp2(a_end - a_col)], axis=0)
    head_of_row = lax.broadcasted_iota(jnp.int32, (LANES, 1), 0)
    head_of_lane = lax.broadcasted_iota(jnp.int32, (1, SSM_GX), 1) // SSM_HEAD_DIM
    spread = jnp.where(head_of_row == head_of_lane, 1.0, 0.0).astype(BF16)
    hi = stacked.astype(BF16)
    lo = (stacked - hi.astype(F32)).astype(BF16)
    wide = _dot(hi, spread) + _dot(lo, spread)
    xdt = x * wide[0:L]
    xdt_b = xdt.astype(BF16)

    cb = jnp.where(tri, _dot_nt(cm.astype(BF16), bm.astype(BF16)), 0.0)
    lane_lo = lax.broadcasted_iota(jnp.int32, (1, LANES), 1) < SSM_HEAD_DIM
    y_parts = []
    for j in range(SSM_RANK // 2):
        xp = xdt_b[:, j * LANES:(j + 1) * LANES]
        ys = []
        for r_ in (2 * j, 2 * j + 1):
            diff = a_col[:, r_:r_ + 1] - a_row[r_:r_ + 1, :]
            mm = cb * jnp.exp2(jnp.minimum(diff, 0.0))
            ys.append(_dot(mm.astype(BF16), xp))
        y_parts.append(jnp.where(lane_lo, ys[0], ys[1]))
    y_in = jnp.concatenate(y_parts, axis=1)

    state = state_ref[...]
    y_st = _dot(cm.astype(BF16), state.astype(BF16)) * wide[L:2 * L]
    xw = (xdt * wide[2 * L:3 * L]).astype(BF16)
    state_ref[...] = state * _expand_heads(jnp.exp2(a_end), 1) + _dot(bm.T.astype(BF16), xw)

    y = y_in + y_st + dexp_ref[...] * x
    o_ref[...] = (y * _silu(z_ref[...].astype(F32))).astype(o_ref.dtype)


def ssd_mixer(proj, conv_w, conv_b, dt_bias, a_log, d_skip, *, batch, seq, offs):
    m = proj.shape[0]
    L = SSD_CHUNK
    nc = seq // L
    G = SSM_GROUPS
    xo, bo, co, dto = (offs[k] for k in ("xs", "B", "C", "dt"))

    def pad_heads(p):
        return jnp.pad(p.reshape(G, 1, SSM_RANK), ((0, 0), (0, 0), (0, LANES - SSM_RANK)))

    d_exp = jnp.repeat(d_skip, SSM_HEAD_DIM).reshape(G, 1, SSM_GX)
    cb2 = conv_b.reshape(1, -1)
    row = lambda b, g, c: b * nc + c
    return pl.pallas_call(
        _ssd_kernel,
        grid=(batch, G, nc),
        in_specs=[pl.BlockSpec((L, SSM_GX), lambda b, g, c: (row(b, g, c), xo // SSM_GX + g)),
                  pl.BlockSpec((L, LANES), lambda b, g, c: (row(b, g, c), bo // LANES + g)),
                  pl.BlockSpec((L, LANES), lambda b, g, c: (row(b, g, c), co // LANES + g)),
                  pl.BlockSpec((L, LANES), lambda b, g, c: (row(b, g, c), dto // LANES + g)),
                  pl.BlockSpec((L, SSM_GX), lambda b, g, c: (row(b, g, c), g)),
                  pl.BlockSpec((CONV_W, SSM_GX), lambda b, g, c: (0, g)),
                  pl.BlockSpec((CONV_W, LANES), lambda b, g, c: (0, W_TOK // LANES + g)),
                  pl.BlockSpec((CONV_W, LANES), lambda b, g, c: (0, (W_TOK + G * SSM_STATE) // LANES + g)),
                  pl.BlockSpec((1, SSM_GX), lambda b, g, c: (0, g)),
                  pl.BlockSpec((1, LANES), lambda b, g, c: (0, W_TOK // LANES + g)),
                  pl.BlockSpec((1, LANES), lambda b, g, c: (0, (W_TOK + G * SSM_STATE) // LANES + g)),
                  pl.BlockSpec((None, 1, LANES), lambda b, g, c: (g, 0, 0)),
                  pl.BlockSpec((None, 1, LANES), lambda b, g, c: (g, 0, 0)),
                  pl.BlockSpec((None, 1, SSM_GX), lambda b, g, c: (g, 0, 0))],
        out_specs=pl.BlockSpec((L, SSM_GX), lambda b, g, c: (row(b, g, c), g)),
        out_shape=jax.ShapeDtypeStruct((m, W_TOK), BF16),
        scratch_shapes=[pltpu.VMEM((SSM_STATE, SSM_GX), F32),
                        pltpu.VMEM((SSD_HALO, SSM_GX + 2 * SSM_STATE), BF16)],
        compiler_params=_cparams(("parallel", "parallel", "arbitrary")),
        name="ssd_mixer",
    )(proj, proj, proj, proj, proj, conv_w, conv_w, conv_w, cb2, cb2, cb2,
      pad_heads(dt_bias), pad_heads(a_log), d_exp)


def _rope_table_kernel(pos_ref, inv_ref, cos_ref, sin_ref):
    ang = pos_ref[...].astype(F32) * inv_ref[...]
    lane = lax.broadcasted_iota(jnp.int32, (1, NSA_HEAD_DIM), 1)
    cos_ref[...] = jnp.cos(ang)
    sin_ref[...] = jnp.where(lane < NSA_HEAD_DIM // 2, -1.0, 1.0) * jnp.sin(ang)


def rope_tables(positions, *, tm):
    m = positions.size
    inv = ROPE_THETA ** (-jnp.arange(0, NSA_HEAD_DIM, 2, dtype=F32) / NSA_HEAD_DIM)
    inv2 = jnp.concatenate([inv, inv]).reshape(1, NSA_HEAD_DIM)
    return pl.pallas_call(
        _rope_table_kernel,
        grid=(m // tm,),
        in_specs=[pl.BlockSpec((tm, 1), lambda i: (i, 0)),
                  pl.BlockSpec((1, NSA_HEAD_DIM), lambda i: (0, 0))],
        out_specs=[pl.BlockSpec((tm, NSA_HEAD_DIM), lambda i: (i, 0))] * 2,
        out_shape=[jax.ShapeDtypeStruct((m, NSA_HEAD_DIM), F32)] * 2,
        compiler_params=_cparams(("parallel",)),
        name="rope_tables",
    )(positions.reshape(m, 1), inv2)


def _nsa_prep_kernel(q_ref, kv_ref, cos_ref, sin_ref, qn_ref, kn_ref,
                     qt_ref, cmp_ref, ks_ref, vst_ref, kw_ref, vwt_ref, *, tm):
    d = NSA_HEAD_DIM
    cos = cos_ref[...]
    sin = sin_ref[...]

    def norm_rope(x, gain, scale):
        x = x.astype(F32)
        x = x * lax.rsqrt(jnp.mean(x * x, axis=-1, keepdims=True) + EPS) * gain
        return (x * cos + pltpu.roll(x, d // 2, 1) * sin) * scale

    for h in range(NSA_KV_HEADS):
        for g in range(NSA_GROUP):
            sl = slice((h * NSA_GROUP + g) * d, (h * NSA_GROUP + g + 1) * d)
            q = norm_rope(q_ref[:, sl], qn_ref[...], d ** -0.5 * LOG2E)
            qt_ref[h, :, g * tm:(g + 1) * tm] = q.T.astype(qt_ref.dtype)
    for h in range(NSA_KV_HEADS):
        ksl = lambda br: slice((2 * br) * NSA_KV_W + h * d, (2 * br) * NSA_KV_W + (h + 1) * d)
        vsl = lambda br: slice((2 * br + 1) * NSA_KV_W + h * d, (2 * br + 1) * NSA_KV_W + (h + 1) * d)
        cmp_ref[:, h * d:(h + 1) * d] = norm_rope(kv_ref[:, ksl(0)], kn_ref[...], 1.0).astype(cmp_ref.dtype)
        cmp_ref[:, NSA_KV_W + h * d:NSA_KV_W + (h + 1) * d] = kv_ref[:, vsl(0)].astype(cmp_ref.dtype)
        for br, k_out, vt_out in ((1, ks_ref, vst_ref), (2, kw_ref, vwt_ref)):
            k_out[:, h * d:(h + 1) * d] = norm_rope(kv_ref[:, ksl(br)], kn_ref[...], 1.0).astype(k_out.dtype)
            vt_out[h] = kv_ref[:, vsl(br)].astype(F32).T.astype(vt_out.dtype)


def nsa_prep(proj, cos, sin, qn, kn, *, offs, tm):
    m = proj.shape[0]
    assert offs["q"] == 0 and offs["kv"] == W_TOK
    d = NSA_HEAD_DIM
    row = lambda w: pl.BlockSpec((tm, w), lambda i: (i, 0))
    vt = pl.BlockSpec((NSA_KV_HEADS, d, tm), lambda i: (0, 0, i))
    return pl.pallas_call(
        functools.partial(_nsa_prep_kernel, tm=tm),
        grid=(m // tm,),
        in_specs=[pl.BlockSpec((tm, W_TOK), lambda i: (i, 0)),
                  pl.BlockSpec((tm, W_TOK), lambda i: (i, 1)),
                  row(d), row(d),
                  pl.BlockSpec((1, d), lambda i: (0, 0)),
                  pl.BlockSpec((1, d), lambda i: (0, 0))],
        out_specs=[pl.BlockSpec((None, NSA_KV_HEADS, d, NSA_GROUP * tm), lambda i: (i, 0, 0, 0)),
                   row(2 * NSA_KV_W), row(NSA_KV_W), vt, row(NSA_KV_W), vt],
        out_shape=[jax.ShapeDtypeStruct((m // tm, NSA_KV_HEADS, d, NSA_GROUP * tm), BF16),
                   jax.ShapeDtypeStruct((m, 2 * NSA_KV_W), BF16),
                   jax.ShapeDtypeStruct((m, NSA_KV_W), BF16),
                   jax.ShapeDtypeStruct((NSA_KV_HEADS, d, m), BF16),
                   jax.ShapeDtypeStruct((m, NSA_KV_W), BF16),
                   jax.ShapeDtypeStruct((NSA_KV_HEADS, d, m), BF16)],
        compiler_params=_cparams(("parallel",)),
        name="nsa_prep",
    )(proj, proj, cos, sin, qn.reshape(1, -1), kn.reshape(1, -1))


def _nsa_compress_kernel(r_ref, pos_ref, kw1_ref, kw2_ref, vw1_ref, vw2_ref, kc_ref, vct_ref):
    d = NSA_HEAD_DIM
    half = CMP_STRIDE * d
    n = r_ref.shape[0]
    posb = jnp.broadcast_to(pos_ref[...], (8, CMP_BLOCK * d)).astype(BF16)
    for off, w1_ref, w2_ref, out, transposed in ((0, kw1_ref, kw2_ref, kc_ref, False),
                                                 (NSA_KV_W, vw1_ref, vw2_ref, vct_ref, True)):
        pos_term = _dot(posb, w1_ref[...])[0:1]
        for h in range(NSA_KV_HEADS):
            cat = jnp.concatenate(
                [r_ref[:, tt * 2 * NSA_KV_W + off + h * d: tt * 2 * NSA_KV_W + off + (h + 1) * d]
                 for tt in range(CMP_STRIDE)], axis=1)
            ha = _dot(cat, w1_ref[0:half, :])
            hb = _dot(cat, w1_ref[half:2 * half, :])
            hsum = ha + pltpu.roll(hb, n - 1, 0) + pos_term
            res = _dot(_silu(hsum).astype(BF16), w2_ref[...])
            out[h] = (res.T if transposed else res).astype(out.dtype)


def nsa_compress(cmp_in, cmp_pos, kw1, kw2, vw1, vw2, *, batch, seq):
    n = seq // CMP_STRIDE
    wide = CMP_STRIDE * 2 * NSA_KV_W
    d = NSA_HEAD_DIM
    r = cmp_in.reshape(batch * n, wide)
    full = lambda shape: pl.BlockSpec(shape, lambda b: (0,) * len(shape))
    return pl.pallas_call(
        _nsa_compress_kernel,
        grid=(batch,),
        in_specs=[pl.BlockSpec((n, wide), lambda b: (b, 0)),
                  full((1, CMP_BLOCK * d)), full((CMP_BLOCK * d, d)), full((d, d)),
                  full((CMP_BLOCK * d, d)), full((d, d))],
        out_specs=[pl.BlockSpec((None, NSA_KV_HEADS, n, d), lambda b: (b, 0, 0, 0)),
                   pl.BlockSpec((None, NSA_KV_HEADS, d, n), lambda b: (b, 0, 0, 0))],
        out_shape=[jax.ShapeDtypeStruct((batch, NSA_KV_HEADS, n, d), BF16),
                   jax.ShapeDtypeStruct((batch, NSA_KV_HEADS, d, n), BF16)],
        compiler_params=_cparams(("parallel",)),
        name="nsa_compress",
    )(r, cmp_pos.reshape(1, CMP_BLOCK * d), kw1.astype(BF16), kw2.astype(BF16), vw1.astype(BF16), vw2.astype(BF16))


def _store_gated_t(o_t, g, gate_t, o_ref, branch, others=(), out_gate_ref=None):
    lane = branch * NSA_GROUP + g
    sl = slice(g * NSA_HEAD_DIM, (g + 1) * NSA_HEAD_DIM)
    gate = jax.nn.sigmoid(gate_t[lane:lane + 1, :])
    o = (o_t * gate).T
    for ref in others:
        o = o + ref[:, sl].astype(F32)
    if out_gate_ref is not None:
        o = o * _silu(out_gate_ref[:, sl].astype(F32))
    o_ref[:, sl] = o.astype(o_ref.dtype)


def _nsa_cmp_kernel(qt_ref, kc_ref, vct_ref, gl_ref, o_ref, selt_ref, *, tq, n_sel):
    i = pl.program_id(2)
    nc = kc_ref.shape[0]
    t = i * tq + lax.broadcasted_iota(jnp.int32, (1, tq), 1)
    cend = lax.broadcasted_iota(jnp.int32, (nc, 1), 0) * CMP_STRIDE + (CMP_BLOCK - 1)
    cmask = cend <= t
    any_vis = jnp.where(t >= CMP_BLOCK - 1, 1.0, 0.0)
    gate_t = gl_ref[...].astype(F32).T
    s_all = _dot(kc_ref[...], qt_ref[...])
    vct = vct_ref[...]
    psum = jnp.zeros((nc, tq), F32)
    for g in range(NSA_GROUP):
        s = jnp.where(cmask, s_all[:, g * tq:(g + 1) * tq], NEG_INF)
        e = jnp.exp2(s - jnp.max(s, axis=0, keepdims=True))
        p = e * (any_vis / jnp.sum(e, axis=0, keepdims=True))
        psum = psum + p
        _store_gated_t(_dot(vct, p.astype(BF16)), g, gate_t, o_ref, 0)

    js = lax.broadcasted_iota(jnp.int32, (n_sel, nc), 0) * SEL_BLOCK
    cs = lax.broadcasted_iota(jnp.int32, (n_sel, nc), 1) * CMP_STRIDE
    agg_t = jnp.where((cs < js + SEL_BLOCK) & (cs + CMP_BLOCK - 1 >= js), 1.0, 0.0).astype(BF16)
    hi, mid, lo = _split3(psum)
    imp = _dot(agg_t, hi) + _dot(agg_t, mid) + _dot(agg_t, lo)
    j = lax.broadcasted_iota(jnp.int32, (n_sel, 1), 0)
    bt = t // SEL_BLOCK
    forced = (j == 0) | (j == bt) | (j == bt - 1)
    sc = jnp.where(forced, SEL_BIG, jnp.where(j <= bt, imp, -SEL_BIG))
    rowi = lax.broadcasted_iota(jnp.int32, (n_sel, tq), 0).astype(F32)
    sel = jnp.zeros((n_sel, tq), F32)
    for _ in range(min(SEL_TOPK, n_sel)):
        mx = jnp.max(sc, axis=0, keepdims=True)
        idx = jnp.min(jnp.where(sc == mx, rowi, float(n_sel)), axis=0, keepdims=True)
        pick = rowi == idx
        sel = jnp.where(pick, 1.0, sel)
        sc = jnp.where(pick, -jnp.inf, sc)
    selt_ref[...] = sel


def nsa_cmp_attn(q_t, kc, vct, proj, *, batch, seq, gl_off, tq):
    m = batch * seq
    nq = seq // tq
    n_cmp = kc.shape[2]
    n_sel = seq // SEL_BLOCK
    d = NSA_HEAD_DIM
    gw = NSA_GROUP * d
    glb = gl_off // LANES
    return pl.pallas_call(
        functools.partial(_nsa_cmp_kernel, tq=tq, n_sel=n_sel),
        grid=(batch, NSA_KV_HEADS, nq),
        in_specs=[pl.BlockSpec((None, None, d, NSA_GROUP * tq), lambda b, h, i: (b * nq + i, h, 0, 0)),
                  pl.BlockSpec((None, None, n_cmp, d), lambda b, h, i: (b, h, 0, 0)),
                  pl.BlockSpec((None, None, d, n_cmp), lambda b, h, i: (b, h, 0, 0)),
                  pl.BlockSpec((tq, LANES), lambda b, h, i: (b * nq + i, glb + h))],
        out_specs=[pl.BlockSpec((tq, gw), lambda b, h, i: (b * nq + i, h)),
                   pl.BlockSpec((None, None, n_sel, tq), lambda b, h, i: (b, h, 0, i))],
        out_shape=[jax.ShapeDtypeStruct((m, W_TOK), BF16),
                   jax.ShapeDtypeStruct((batch, NSA_KV_HEADS, n_sel, seq), F32)],
        compiler_params=_cparams(("parallel", "parallel", "parallel")),
        name="nsa_cmp_attn",
    )(q_t, kc, vct, proj)


def _nsa_flash_kernel(qi_ref, ki_ref, first_ref, last_ref, qt_ref, k_ref, vt_ref, *rest, tq, tk, branch):
    if branch == 1:
        selt_ref, gl_ref, o_ref, m_ref, acc_ref = rest
        others, out_gate_ref = (), None
    else:
        gl_ref, ocmp_ref, osel_ref, out_gate_ref, o_ref, m_ref, acc_ref = rest
        others = (ocmp_ref, osel_ref)
    step = pl.program_id(2)
    qi = qi_ref[step]
    ki = ki_ref[step]

    @pl.when(first_ref[step] == 1)
    def _():
        m_ref[...] = jnp.full_like(m_ref, NEG_INF)
        acc_ref[...] = jnp.zeros_like(acc_ref)

    t = qi * tq + lax.broadcasted_iota(jnp.int32, (1, tq), 1)
    kp = ki * tk + lax.broadcasted_iota(jnp.int32, (tk, 1), 0)
    if branch == 1:
        per_tile = tk // SEL_BLOCK
        per_load = 8 // per_tile
        rows8 = selt_ref[pl.ds(pl.multiple_of((ki // per_load) * 8, 8), 8), :]
        sub = ki % per_load
        rows = rows8[0:per_tile]
        for u in range(1, per_load):
            rows = jnp.where(sub == u, rows8[u * per_tile:(u + 1) * per_tile], rows)
        selm = jnp.concatenate([jnp.broadcast_to(rows[u:u + 1], (SEL_BLOCK, tq)) for u in range(per_tile)], axis=0)
        mask = (selm > 0.5) & (kp <= t)
    else:
        mask = (kp <= t) & (kp > t - WINDOW)
    bias = jnp.where(mask, 0.0, -jnp.inf)
    s_all = _dot(k_ref[...], qt_ref[...]) + jnp.concatenate([bias] * NSA_GROUP, axis=1)
    vt = jnp.concatenate([vt_ref[...], jnp.ones((ROWSUM_ROWS, tk), BF16)], axis=0)
    m_all = m_ref[...]
    m_rows = []
    for g in range(NSA_GROUP):
        cols = slice(g * tq, (g + 1) * tq)
        s = s_all[:, cols]
        m_old = m_all[g:g + 1, :]
        m_new = jnp.maximum(m_old, jnp.max(s, axis=0, keepdims=True))
        alpha = jnp.exp2(m_old - m_new)
        p = jnp.exp2((s - m_new).astype(BF16))
        m_rows.append(m_new)
        acc_ref[:, cols] = alpha * acc_ref[:, cols] + _dot(vt, p)
    m_ref[...] = jnp.concatenate(m_rows, axis=0)

    @pl.when(last_ref[step] == 1)
    def _():
        gate_t = gl_ref[...].astype(F32).T
        d = NSA_HEAD_DIM
        for g in range(NSA_GROUP):
            cols = slice(g * tq, (g + 1) * tq)
            o_t = acc_ref[0:d, cols] / acc_ref[d:d + 1, cols]
            _store_gated_t(o_t, g, gate_t, o_ref, branch, others, out_gate_ref)


def nsa_flash_attn(q_t, k, v_t, proj, *, batch, seq, gl_off, branch, tq, tk, sel_t=None, others=(), gate_off=None):
    m = batch * seq
    nq = seq // tq
    nk = seq // tk
    d = NSA_HEAD_DIM
    gw = NSA_GROUP * d
    glb = gl_off // LANES
    n_sel = seq // SEL_BLOCK
    assert 8 % (tk // SEL_BLOCK) == 0 and n_sel % 8 == 0
    steps = []
    for i in range(nq):
        hi_k = ((i + 1) * tq - 1) // tk
        lo_k = 0 if branch == 1 else max(0, (i * tq - WINDOW + 1) // tk)
        for kk in range(lo_k, hi_k + 1):
            steps.append((i, kk, int(kk == lo_k), int(kk == hi_k)))
    tabs = [jnp.asarray(np.array([s_[c] for s_ in steps], np.int32)) for c in range(4)]
    q_tile = lambda col: pl.BlockSpec((tq, gw), lambda b, h, s, qi, ki, fi, la: (b * nq + qi[s], col + h))
    gl_spec = pl.BlockSpec((tq, LANES), lambda b, h, s, qi, ki, fi, la: (b * nq + qi[s], glb + h))
    in_specs = [pl.BlockSpec((None, None, d, NSA_GROUP * tq), lambda b, h, s, qi, ki, fi, la: (b * nq + qi[s], h, 0, 0)),
                pl.BlockSpec((tk, d), lambda b, h, s, qi, ki, fi, la: (b * nk + ki[s], h)),
                pl.BlockSpec((None, d, tk), lambda b, h, s, qi, ki, fi, la: (h, 0, b * nk + ki[s]))]
    if branch == 1:
        in_specs += [pl.BlockSpec((None, None, n_sel, tq), lambda b, h, s, qi, ki, fi, la: (b, h, 0, qi[s])), gl_spec]
        operands = (sel_t, proj)
    else:
        assert gate_off % gw == 0
        in_specs += [gl_spec, q_tile(0), q_tile(0), q_tile(gate_off // gw)]
        operands = (proj, *others, proj)
    grid_spec = pltpu.PrefetchScalarGridSpec(
        num_scalar_prefetch=4,
        grid=(batch, NSA_KV_HEADS, len(steps)),
        in_specs=in_specs,
        out_specs=q_tile(0),
        scratch_shapes=[pltpu.VMEM((NSA_GROUP, tq), F32),
                        pltpu.VMEM((d + ROWSUM_ROWS, NSA_GROUP * tq), F32)])
    return pl.pallas_call(
        functools.partial(_nsa_flash_kernel, tq=tq, tk=tk, branch=branch),
        grid_spec=grid_spec,
        out_shape=jax.ShapeDtypeStruct((m, W_TOK), BF16),
        compiler_params=_cparams(("parallel", "parallel", "arbitrary")),
        name="nsa_sel_attn" if branch == 1 else "nsa_win_attn",
    )(*tabs, q_t, k, v_t, *operands)


def _nsa_layout(w_in):
    kv_end = W_TOK + 6 * NSA_KV_W
    gl_end = kv_end + 3 * NSA_HEADS
    k = w_in.shape[0]
    glw = w_in[:, kv_end:gl_end].reshape(k, NSA_KV_HEADS, NSA_GROUP, 3)
    glw = glw.transpose(0, 1, 3, 2).reshape(k, NSA_KV_HEADS, 3 * NSA_GROUP)
    glw = jnp.pad(glw, ((0, 0), (0, 0), (0, LANES - 3 * NSA_GROUP))).reshape(k, NSA_KV_HEADS * LANES)
    pad_w = 2 * W_TOK - kv_end - NSA_KV_HEADS * LANES
    w = jnp.concatenate([w_in[:, :kv_end], glw, jnp.zeros((k, pad_w), w_in.dtype), w_in[:, gl_end:]], axis=1)
    offs = {"q": 0, "kv": W_TOK, "gl": kv_end, "gate": 2 * W_TOK, "memq": 3 * W_TOK}
    return w.astype(BF16), offs


def _ssd_layout(w_in):
    k = w_in.shape[0]
    xbc_end = W_TOK + W_TOK + 2 * SSM_GROUPS * SSM_STATE
    dt_end = xbc_end + SSM_HEADS
    dtw = w_in[:, xbc_end:dt_end].reshape(k, SSM_GROUPS, SSM_RANK)
    dtw = jnp.pad(dtw, ((0, 0), (0, 0), (0, LANES - SSM_RANK))).reshape(k, SSM_GROUPS * LANES)
    w = jnp.concatenate([w_in[:, :xbc_end], w_in[:, dt_end:], dtw], axis=1)
    offs = {"z": 0, "xs": W_TOK, "B": 2 * W_TOK, "C": 2 * W_TOK + SSM_GROUPS * SSM_STATE,
            "memq": xbc_end, "dt": xbc_end + 2 * W_MEM}
    return w.astype(BF16), offs


def nsa_mixer(proj, positions, qn, kn, cmp_pos, kw1, kw2, vw1, vw2, *, batch, seq, offs):
    cos, sin = rope_tables(positions, tm=min(1024, batch * seq))
    tq = 256
    common = dict(batch=batch, seq=seq, gl_off=offs["gl"])
    q_t, cmp_in, k_s, vt_s, k_w, vt_w = nsa_prep(proj, cos, sin, qn, kn, offs=offs, tm=tq)
    kc, vct = nsa_compress(cmp_in, cmp_pos, kw1, kw2, vw1, vw2, batch=batch, seq=seq)
    o_cmp, sel_t = nsa_cmp_attn(q_t, kc, vct, proj, tq=tq, **common)
    o_sel = nsa_flash_attn(q_t, k_s, vt_s, proj, branch=1, tq=tq, tk=min(512, seq), sel_t=sel_t, **common)
    return nsa_flash_attn(q_t, k_w, vt_w, proj, branch=2, tq=tq, tk=256, others=(o_cmp, o_sel),
                          gate_off=offs["gate"], **common)


def _layer(x2, mem2, mem_norm, norm, w_in_b, w_out, mem_wkv, mem_qn, mem_kn, mixer, memq_off, *, seq, gnorm=None):
    m = x2.shape[0]
    proj = norm_matmul(x2, norm, w_in_b, tm=_tile(m, 1024), tn=_tile(w_in_b.shape[1], 1024), out_dtype=BF16)
    kv = norm_matmul(mem2, mem_norm, mem_wkv.astype(BF16), tm=_tile(mem2.shape[0], 512), tn=512, out_dtype=F32)
    y_mem = mem_attn(proj, kv, mem_qn, mem_kn, q_off=memq_off, seq=seq, tq=min(512, seq))
    y_tok = mixer(proj)
    w_tok = w_out[:W_TOK].astype(BF16)
    w_mem = w_out[W_TOK:].astype(BF16)
    norm_tok = gnorm is not None
    gain = gnorm if norm_tok else jnp.ones((W_TOK,), F32)
    tm, tn = (_tile(m, 512), 1024) if norm_tok else (_tile(m, 1024), 512)
    return out_proj(y_tok, y_mem, w_tok, w_mem, x2, gain, norm_tok=norm_tok, tm=tm, tn=tn)


def kernel(x, mem, positions, mem_norm, l0_norm, l0_w_in, l0_w_out, l0_mem_wkv, l0_mem_qnorm, l0_mem_knorm, l0_pool_w, l0_pool_scale, l1_norm, l1_w_in, l1_w_out, l1_mem_wkv, l1_mem_qnorm, l1_mem_knorm, l1_qnorm, l1_knorm, l1_cmp_pos, l1_cmp_k_w1, l1_cmp_k_w2, l1_cmp_v_w1, l1_cmp_v_w2, l2_norm, l2_w_in, l2_w_out, l2_mem_wkv, l2_mem_qnorm, l2_mem_knorm, l2_conv_w, l2_conv_b, l2_dt_bias, l2_A_log, l2_D, l2_gnorm, l3_norm, l3_w_in, l3_w_out, l3_mem_wkv, l3_mem_qnorm, l3_mem_knorm, l3_v_norm, l3_sgu_w, l3_sgu_b):
    batch, seq, d = x.shape
    x2 = x.reshape(batch * seq, d)
    mem2 = mem.reshape(batch * MEM_TOKENS, d)

    x2 = _layer(x2, mem2, mem_norm, l0_norm, l0_w_in.astype(BF16), l0_w_out, l0_mem_wkv, l0_mem_qnorm, l0_mem_knorm,
                lambda p: pool_mixer(p, l0_pool_w, l0_pool_scale, seq=seq, tm=256), 2 * W_TOK, seq=seq)

    w1, offs1 = _nsa_layout(l1_w_in)
    x2 = _layer(x2, mem2, mem_norm, l1_norm, w1, l1_w_out, l1_mem_wkv, l1_mem_qnorm, l1_mem_knorm,
                lambda p: nsa_mixer(p, positions, l1_qnorm, l1_knorm, l1_cmp_pos, l1_cmp_k_w1, l1_cmp_k_w2,
                                    l1_cmp_v_w1, l1_cmp_v_w2, batch=batch, seq=seq, offs=offs1),
                offs1["memq"], seq=seq)

    w2, offs2 = _ssd_layout(l2_w_in)
    x2 = _layer(x2, mem2, mem_norm, l2_norm, w2, l2_w_out, l2_mem_wkv, l2_mem_qnorm, l2_mem_knorm,
                lambda p: ssd_mixer(p, l2_conv_w, l2_conv_b, l2_dt_bias, l2_A_log, l2_D, batch=batch, seq=seq, offs=offs2),
                offs2["memq"], seq=seq, gnorm=l2_gnorm)

    x2 = _layer(x2, mem2, mem_norm, l3_norm, l3_w_in.astype(BF16), l3_w_out, l3_mem_wkv, l3_mem_qnorm, l3_mem_knorm,
                lambda p: sgu_mixer(p, l3_v_norm, l3_sgu_w, l3_sgu_b, tm=256), 3 * W_TOK, seq=seq)
    return x2.reshape(batch, seq, d)
```

```python
import functools
import math

import numpy as np
import jax
import jax.numpy as jnp
from jax import lax
from jax.experimental import pallas as pl
from jax.experimental.pallas import tpu as pltpu

F32 = jnp.float32
BF16 = jnp.bfloat16

D_MODEL = 2048
EPS = 1e-6
ROPE_THETA = 10000.0
NEG_INF = -1e30
SEL_BIG = 1e9
LOG2E = 1.0 / math.log(2.0)

W_TOK = 2 * D_MODEL
MEM_TOKENS = 256
MEM_HEADS = 4
MEM_HEAD_DIM = D_MODEL // 8
W_MEM = MEM_HEADS * MEM_HEAD_DIM

POOL_GROUPS = 4
POOL_GW = W_TOK // POOL_GROUPS
POOL_HALO = 16

NSA_HEADS = 32
NSA_KV_HEADS = 4
NSA_GROUP = NSA_HEADS // NSA_KV_HEADS
NSA_HEAD_DIM = 128
NSA_KV_W = NSA_KV_HEADS * NSA_HEAD_DIM
CMP_BLOCK = 32
CMP_STRIDE = 16
SEL_BLOCK = 64
SEL_TOPK = 16
WINDOW = 512
ROWSUM_ROWS = 16

SSM_HEAD_DIM = 64
SSM_HEADS = W_TOK // SSM_HEAD_DIM
SSM_GROUPS = 8
SSM_RANK = SSM_HEADS // SSM_GROUPS
SSM_STATE = 128
CONV_W = 4
SSD_CHUNK = 256
SSM_GX = W_TOK // SSM_GROUPS
SSD_HALO = 16

SGU_CHUNK = 128
SGU_GROUPS = 8
SGU_GW = W_TOK // SGU_GROUPS

LANES = 128
VMEM_LIMIT = 56 * 1024 * 1024


def _cparams(sem):
    return pltpu.CompilerParams(dimension_semantics=sem, vmem_limit_bytes=VMEM_LIMIT)


def _tile(n, pref):
    t = pref
    while n % t:
        t //= 2
    return t


def _silu(x):
    return x * jax.nn.sigmoid(x)


def _gelu_exact(x):
    return 0.5 * x * (1.0 + lax.erf(x * (1.0 / math.sqrt(2.0))))


def _dot(a, b):
    return jnp.dot(a, b, preferred_element_type=F32)


def _dot_nt(a, b):
    return lax.dot_general(a, b, (((1,), (1,)), ((), ())), preferred_element_type=F32)


def _split3(x):
    hi = x.astype(BF16)
    r1 = x - hi.astype(F32)
    mid = r1.astype(BF16)
    lo = (r1 - mid.astype(F32)).astype(BF16)
    return hi, mid, lo


def _norm_matmul_kernel(x_ref, g_ref, w_ref, o_ref, h_ref):
    @pl.when(pl.program_id(1) == 0)
    def _():
        x = x_ref[...].astype(F32)
        ms = jnp.mean(x * x, axis=-1, keepdims=True)
        h_ref[...] = (x * lax.rsqrt(ms + EPS) * g_ref[...]).astype(BF16)

    o_ref[...] = _dot(h_ref[...], w_ref[...]).astype(o_ref.dtype)


def norm_matmul(x, g, w, *, tm, tn, out_dtype):
    m, k = x.shape
    n = w.shape[1]
    return pl.pallas_call(
        _norm_matmul_kernel,
        grid=(m // tm, n // tn),
        in_specs=[pl.BlockSpec((tm, k), lambda i, j: (i, 0)),
                  pl.BlockSpec((1, k), lambda i, j: (0, 0)),
                  pl.BlockSpec((k, tn), lambda i, j: (0, j))],
        out_specs=pl.BlockSpec((tm, tn), lambda i, j: (i, j)),
        out_shape=jax.ShapeDtypeStruct((m, n), out_dtype),
        scratch_shapes=[pltpu.VMEM((tm, k), BF16)],
        compiler_params=_cparams(("parallel", "arbitrary")),
        name="norm_matmul",
    )(x, g.reshape(1, k), w)


def _out_proj_kernel(yt_ref, ym_ref, wt_ref, wm_ref, x_ref, g_ref, o_ref, *scratch, norm_tok):
    if norm_tok:
        (h_ref,) = scratch

        @pl.when(pl.program_id(1) == 0)
        def _():
            y = yt_ref[...].astype(F32)
            ms = jnp.mean(y * y, axis=-1, keepdims=True)
            h_ref[...] = (y * lax.rsqrt(ms + EPS) * g_ref[...]).astype(BF16)

        yt = h_ref[...]
    else:
        yt = yt_ref[...]
    acc = _dot(yt, wt_ref[...]) + _dot(ym_ref[...], wm_ref[...])
    o_ref[...] = x_ref[...] + acc


def out_proj(y_tok, y_mem, w_tok, w_mem, x, gain, *, norm_tok, tm, tn):
    m, kt = y_tok.shape
    km = y_mem.shape[1]
    n = w_tok.shape[1]
    return pl.pallas_call(
        functools.partial(_out_proj_kernel, norm_tok=norm_tok),
        grid=(m // tm, n // tn),
        in_specs=[pl.BlockSpec((tm, kt), lambda i, j: (i, 0)),
                  pl.BlockSpec((tm, km), lambda i, j: (i, 0)),
                  pl.BlockSpec((kt, tn), lambda i, j: (0, j)),
                  pl.BlockSpec((km, tn), lambda i, j: (0, j)),
                  pl.BlockSpec((tm, tn), lambda i, j: (i, j)),
                  pl.BlockSpec((1, kt), lambda i, j: (0, 0))],
        out_specs=pl.BlockSpec((tm, tn), lambda i, j: (i, j)),
        out_shape=jax.ShapeDtypeStruct((m, n), F32),
        scratch_shapes=[pltpu.VMEM((tm, kt), BF16)] if norm_tok else [],
        compiler_params=_cparams(("parallel", "arbitrary")),
        name="out_proj",
    )(y_tok, y_mem, w_tok, w_mem, x, gain.reshape(1, kt))


def _mem_attn_kernel(q_ref, gate_ref, kv_ref, qn_ref, kn_ref, o_ref):
    hd = MEM_HEAD_DIM
    scale = hd ** -0.5
    for h in range(MEM_HEADS):
        q = q_ref[:, h * hd:(h + 1) * hd].astype(F32)
        q = q * lax.rsqrt(jnp.mean(q * q, axis=-1, keepdims=True) + EPS) * qn_ref[...]
        k = kv_ref[:, h * hd:(h + 1) * hd].astype(F32)
        k = k * lax.rsqrt(jnp.mean(k * k, axis=-1, keepdims=True) + EPS) * kn_ref[...]
        v = kv_ref[:, W_MEM + h * hd:W_MEM + (h + 1) * hd].astype(BF16)
        s = _dot_nt((q * scale).astype(BF16), k.astype(BF16))
        e = jnp.exp(s - jnp.max(s, axis=-1, keepdims=True))
        o = _dot(e.astype(BF16), v) / jnp.sum(e, axis=-1, keepdims=True)
        gate = gate_ref[:, h * hd:(h + 1) * hd].astype(F32)
        o_ref[:, h * hd:(h + 1) * hd] = (o * _silu(gate)).astype(o_ref.dtype)


def mem_attn(proj, kv, qn, kn, *, q_off, seq, tq):
    m = proj.shape[0]
    nq = seq // tq
    qb = q_off // W_MEM
    return pl.pallas_call(
        _mem_attn_kernel,
        grid=(m // tq,),
        in_specs=[pl.BlockSpec((tq, W_MEM), lambda i: (i, qb)),
                  pl.BlockSpec((tq, W_MEM), lambda i: (i, qb + 1)),
                  pl.BlockSpec((MEM_TOKENS, 2 * W_MEM), lambda i: (i // nq, 0)),
                  pl.BlockSpec((1, MEM_HEAD_DIM), lambda i: (0, 0)),
                  pl.BlockSpec((1, MEM_HEAD_DIM), lambda i: (0, 0))],
        out_specs=pl.BlockSpec((tq, W_MEM), lambda i: (i, 0)),
        out_shape=jax.ShapeDtypeStruct((m, W_MEM), BF16),
        compiler_params=_cparams(("parallel",)),
        name="mem_attn",
    )(proj, proj, kv, qn.reshape(1, -1), kn.reshape(1, -1))


def _pool_kernel(v_ref, halo_ref, gate_ref, w_ref, scale_ref, o_ref, *, tm, seq):
    g = pl.program_id(0)
    i = pl.program_id(1)
    win = jnp.left_shift(2, g)
    t_seq = lax.rem(i * tm, seq)
    r = lax.broadcasted_iota(jnp.int32, (tm, 1), 0)
    c = lax.broadcasted_iota(jnp.int32, (1, tm), 1)
    a_main = jnp.where((c <= r) & (c > r - win), 1.0, 0.0).astype(BF16)
    ch = lax.broadcasted_iota(jnp.int32, (1, POOL_HALO), 1)
    a_halo = jnp.where((ch > r + POOL_HALO - win) & (t_seq > 0), 1.0, 0.0).astype(BF16)
    v = v_ref[...]
    wsum = _dot(a_main, v.astype(BF16)) + _dot(a_halo, halo_ref[...].astype(BF16))
    cnt = jnp.minimum(t_seq + r + 1, win).astype(F32)
    mix = wsum / cnt - v.astype(F32)
    out = _dot(mix.astype(BF16), w_ref[...]) * scale_ref[...]
    o_ref[...] = (out * _silu(gate_ref[...].astype(F32))).astype(o_ref.dtype)


def pool_mixer(proj, pool_w, pool_scale, *, seq, tm):
    m = proj.shape[0]
    hb = tm // POOL_HALO
    return pl.pallas_call(
        functools.partial(_pool_kernel, tm=tm, seq=seq),
        grid=(POOL_GROUPS, m // tm),
        in_specs=[pl.BlockSpec((tm, POOL_GW), lambda g, i: (i, g)),
                  pl.BlockSpec((POOL_HALO, POOL_GW), lambda g, i: (jnp.maximum(i * hb - 1, 0), g)),
                  pl.BlockSpec((tm, POOL_GW), lambda g, i: (i, POOL_GROUPS + g)),
                  pl.BlockSpec((None, POOL_GW, POOL_GW), lambda g, i: (g, 0, 0)),
                  pl.BlockSpec((1, POOL_GW), lambda g, i: (0, g))],
        out_specs=pl.BlockSpec((tm, POOL_GW), lambda g, i: (i, g)),
        out_shape=jax.ShapeDtypeStruct((m, W_TOK), BF16),
        compiler_params=_cparams(("parallel", "parallel")),
        name="pool_mixer",
    )(proj, proj, proj, pool_w.astype(BF16), pool_scale.reshape(1, W_TOK))


def _sgu_kernel(u_ref, v_ref, gate_ref, vn_ref, w_ref, bt_ref, o_ref, vs_ref, *, tm):
    ssq = jnp.zeros((tm, 1), F32)
    for g in range(SGU_GROUPS):
        sl = slice(g * SGU_GW, (g + 1) * SGU_GW)
        vg = _gelu_exact(v_ref[:, sl].astype(F32))
        vs_ref[:, sl] = vg
        ssq = ssq + jnp.sum(vg * vg, axis=-1, keepdims=True)
    inv = lax.rsqrt(ssq * (1.0 / W_TOK) + EPS)
    r = lax.broadcasted_iota(jnp.int32, (SGU_CHUNK, SGU_CHUNK), 0)
    c = lax.broadcasted_iota(jnp.int32, (SGU_CHUNK, SGU_CHUNK), 1)
    for g in range(SGU_GROUPS):
        sl = slice(g * SGU_GW, (g + 1) * SGU_GW)
        w = jnp.where(r >= c, w_ref[g], 0.0).astype(BF16)
        vn = (vs_ref[:, sl] * inv * vn_ref[:, sl]).astype(BF16)
        for ck in range(tm // SGU_CHUNK):
            rs = slice(ck * SGU_CHUNK, (ck + 1) * SGU_CHUNK)
            mixed = _dot(w, vn[rs]) + bt_ref[:, g:g + 1]
            u = _gelu_exact(u_ref[rs, sl].astype(F32))
            o_ref[rs, sl] = (u * mixed * _silu(gate_ref[rs, sl].astype(F32))).astype(o_ref.dtype)


def sgu_mixer(proj, v_norm, sgu_w, sgu_b, *, tm):
    m = proj.shape[0]
    return pl.pallas_call(
        functools.partial(_sgu_kernel, tm=tm),
        grid=(m // tm,),
        in_specs=[pl.BlockSpec((tm, W_TOK), lambda i: (i, 0)),
                  pl.BlockSpec((tm, W_TOK), lambda i: (i, 1)),
                  pl.BlockSpec((tm, W_TOK), lambda i: (i, 2)),
                  pl.BlockSpec((1, W_TOK), lambda i: (0, 0)),
                  pl.BlockSpec((SGU_GROUPS, SGU_CHUNK, SGU_CHUNK), lambda i: (0, 0, 0)),
                  pl.BlockSpec((SGU_CHUNK, SGU_GROUPS), lambda i: (0, 0))],
        out_specs=pl.BlockSpec((tm, W_TOK), lambda i: (i, 0)),
        out_shape=jax.ShapeDtypeStruct((m, W_TOK), BF16),
        scratch_shapes=[pltpu.VMEM((tm, W_TOK), F32)],
        compiler_params=_cparams(("parallel",)),
        name="sgu_mixer",
    )(proj, proj, proj, v_norm.reshape(1, W_TOK), sgu_w, sgu_b.T)


def _conv_silu(raw, prev, w, b):
    L = raw.shape[0]
    rr = lax.broadcasted_iota(jnp.int32, ((CONV_W - 1) * L, 1), 0)
    src = rr % L - (rr // L + 1)
    main = jnp.where(lax.broadcasted_iota(jnp.int32, (1, L), 1) == src, 1.0, 0.0).astype(BF16)
    shifted = _dot(main, raw)
    row8 = lax.broadcasted_iota(jnp.int32, (8, 1), 0)
    prev32 = prev.astype(F32)
    acc = raw.astype(F32) * w[CONV_W - 1:CONV_W, :] + b
    for k in range(1, CONV_W):
        blk = shifted[(k - 1) * L:k * L]
        head = blk[0:8] + jnp.where(row8 < k, pltpu.roll(prev32, k, 0)[0:8], 0.0)
        blk = jnp.concatenate([head, blk[8:]], axis=0)
        acc = acc + blk * w[CONV_W - 1 - k:CONV_W - k, :]
    return _silu(acc)


def _expand_heads(cols, n_rows):
    lane_head = lax.broadcasted_iota(jnp.int32, (1, SSM_GX), 1) // SSM_HEAD_DIM
    out = jnp.zeros((n_rows, SSM_GX), F32)
    for r in range(SSM_RANK):
        out = jnp.where(lane_head == r, cols[:, r:r + 1], out)
    return out


def _ssd_kernel(x_ref, b_ref, c_ref, dt_ref, z_ref, cwx_ref, cwb_ref, cwc_ref, cbx_ref, cbb_ref, cbc_ref,
                dtb_ref, alog_ref, dexp_ref, o_ref, state_ref, prev_ref):
    L = SSD_CHUNK
    ck = pl.program_id(2)

    @pl.when(ck == 0)
    def _():
        state_ref[...] = jnp.zeros_like(state_ref)
        prev_ref[...] = jnp.zeros_like(prev_ref)

    raw = jnp.concatenate([x_ref[...], b_ref[...], c_ref[...]], axis=1).astype(BF16)
    conv_w = jnp.concatenate([cwx_ref[...], cwb_ref[...], cwc_ref[...]], axis=1)
    conv_b = jnp.concatenate([cbx_ref[...], cbb_ref[...], cbc_ref[...]], axis=1)
    conv = _conv_silu(raw, prev_ref[...], conv_w, conv_b)
    prev_ref[...] = raw[L - SSD_HALO:L]
    x = conv[:, :SSM_GX]
    bm = conv[:, SSM_GX:SSM_GX + SSM_STATE]
    cm = conv[:, SSM_GX + SSM_STATE:]

    dt = jax.nn.softplus(dt_ref[...].astype(F32) + dtb_ref[...])
    lane = lax.broadcasted_iota(jnp.int32, (1, LANES), 1)
    dt = jnp.where(lane < SSM_RANK, dt, 0.0)
    dta = dt * (-jnp.exp(alog_ref[...]) * LOG2E)
    r = lax.broadcasted_iota(jnp.int32, (L, L), 0)
    c = lax.broadcasted_iota(jnp.int32, (L, L), 1)
    tri = r >= c
    tril = jnp.where(tri, 1.0, 0.0).astype(BF16)
    hi, mid, lo = _split3(dta)
    a_col = _dot(tril, hi) + _dot(tril, mid) + _dot(tril, lo)
    a_row = a_col.T
    a_end = a_col[L - 1:L, :]

    stacked = jnp.concatenate([dt, jnp.exp2(a_col), jnp.exp2(a_end - a_col)], axis=0)
    head_of_row = lax.broadcasted_iota(jnp.int32, (LANES, 1), 0)
    head_of_lane = lax.broadcasted_iota(jnp.int32, (1, SSM_GX), 1) // SSM_HEAD_DIM
    spread = jnp.where(head_of_row == head_of_lane, 1.0, 0.0).astype(BF16)
    hi = stacked.astype(BF16)
    lo = (stacked - hi.astype(F32)).astype(BF16)
    wide = _dot(hi, spread) + _dot(lo, spread)
    xdt = x * wide[0:L]
    xdt_b = xdt.astype(BF16)

    cb = jnp.where(tri, _dot_nt(cm.astype(BF16), bm.astype(BF16)), 0.0)
    lane_lo = lax.broadcasted_iota(jnp.int32, (1, LANES), 1) < SSM_HEAD_DIM
    y_parts = []
    for j in range(SSM_RANK // 2):
        xp = xdt_b[:, j * LANES:(j + 1) * LANES]
        ys = []
        for r_ in (2 * j, 2 * j + 1):
            diff = a_col[:, r_:r_ + 1] - a_row[r_:r_ + 1, :]
            mm = cb * jnp.exp2(jnp.minimum(diff, 0.0))
            ys.append(_dot(mm.astype(BF16), xp))
        y_parts.append(jnp.where(lane_lo, ys[0], ys[1]))
    y_in = jnp.concatenate(y_parts, axis=1)

    state = state_ref[...]
    y_st = _dot(cm.astype(BF16), state.astype(BF16)) * wide[L:2 * L]
    xw = (xdt * wide[2 * L:3 * L]).astype(BF16)
    state_ref[...] = state * _expand_heads(jnp.exp2(a_end), 1) + _dot(bm.T.astype(BF16), xw)

    y = y_in + y_st + dexp_ref[...] * x
    o_ref[...] = (y * _silu(z_ref[...].astype(F32))).astype(o_ref.dtype)


def ssd_mixer(proj, conv_w, conv_b, dt_bias, a_log, d_skip, *, batch, seq, offs):
    m = proj.shape[0]
    L = SSD_CHUNK
    nc = seq // L
    G = SSM_GROUPS
    xo, bo, co, dto = (offs[k] for k in ("xs", "B", "C", "dt"))

    def pad_heads(p):
        return jnp.pad(p.reshape(G, 1, SSM_RANK), ((0, 0), (0, 0), (0, LANES - SSM_RANK)))

    d_exp = jnp.repeat(d_skip, SSM_HEAD_DIM).reshape(G, 1, SSM_GX)
    cb2 = conv_b.reshape(1, -1)
    row = lambda b, g, c: b * nc + c
    return pl.pallas_call(
        _ssd_kernel,
        grid=(batch, G, nc),
        in_specs=[pl.BlockSpec((L, SSM_GX), lambda b, g, c: (row(b, g, c), xo // SSM_GX + g)),
                  pl.BlockSpec((L, LANES), lambda b, g, c: (row(b, g, c), bo // LANES + g)),
                  pl.BlockSpec((L, LANES), lambda b, g, c: (row(b, g, c), co // LANES + g)),
                  pl.BlockSpec((L, LANES), lambda b, g, c: (row(b, g, c), dto // LANES + g)),
                  pl.BlockSpec((L, SSM_GX), lambda b, g, c: (row(b, g, c), g)),
                  pl.BlockSpec((CONV_W, SSM_GX), lambda b, g, c: (0, g)),
                  pl.BlockSpec((CONV_W, LANES), lambda b, g, c: (0, W_TOK // LANES + g)),
                  pl.BlockSpec((CONV_W, LANES), lambda b, g, c: (0, (W_TOK + G * SSM_STATE) // LANES + g)),
                  pl.BlockSpec((1, SSM_GX), lambda b, g, c: (0, g)),
                  pl.BlockSpec((1, LANES), lambda b, g, c: (0, W_TOK // LANES + g)),
                  pl.BlockSpec((1, LANES), lambda b, g, c: (0, (W_TOK + G * SSM_STATE) // LANES + g)),
                  pl.BlockSpec((None, 1, LANES), lambda b, g, c: (g, 0, 0)),
                  pl.BlockSpec((None, 1, LANES), lambda b, g, c: (g, 0, 0)),
                  pl.BlockSpec((None, 1, SSM_GX), lambda b, g, c: (g, 0, 0))],
        out_specs=pl.BlockSpec((L, SSM_GX), lambda b, g, c: (row(b, g, c), g)),
        out_shape=jax.ShapeDtypeStruct((m, W_TOK), BF16),
        scratch_shapes=[pltpu.VMEM((SSM_STATE, SSM_GX), F32),
                        pltpu.VMEM((SSD_HALO, SSM_GX + 2 * SSM_STATE), BF16)],
        compiler_params=_cparams(("parallel", "parallel", "arbitrary")),
        name="ssd_mixer",
    )(proj, proj, proj, proj, proj, conv_w, conv_w, conv_w, cb2, cb2, cb2,
      pad_heads(dt_bias), pad_heads(a_log), d_exp)


def _rope_table_kernel(pos_ref, inv_ref, cos_ref, sin_ref):
    ang = pos_ref[...].astype(F32) * inv_ref[...]
    lane = lax.broadcasted_iota(jnp.int32, (1, NSA_HEAD_DIM), 1)
    cos_ref[...] = jnp.cos(ang)
    sin_ref[...] = jnp.where(lane < NSA_HEAD_DIM // 2, -1.0, 1.0) * jnp.sin(ang)


def rope_tables(positions, *, tm):
    m = positions.size
    inv = ROPE_THETA ** (-jnp.arange(0, NSA_HEAD_DIM, 2, dtype=F32) / NSA_HEAD_DIM)
    inv2 = jnp.concatenate([inv, inv]).reshape(1, NSA_HEAD_DIM)
    return pl.pallas_call(
        _rope_table_kernel,
        grid=(m // tm,),
        in_specs=[pl.BlockSpec((tm, 1), lambda i: (i, 0)),
                  pl.BlockSpec((1, NSA_HEAD_DIM), lambda i: (0, 0))],
        out_specs=[pl.BlockSpec((tm, NSA_HEAD_DIM), lambda i: (i, 0))] * 2,
        out_shape=[jax.ShapeDtypeStruct((m, NSA_HEAD_DIM), F32)] * 2,
        compiler_params=_cparams(("parallel",)),
        name="rope_tables",
    )(positions.reshape(m, 1), inv2)


def _nsa_prep_kernel(q_ref, kv_ref, cos_ref, sin_ref, qn_ref, kn_ref,
                     qt_ref, cmp_ref, ks_ref, vst_ref, kw_ref, vwt_ref, *, tm):
    d = NSA_HEAD_DIM
    r = lax.broadcasted_iota(jnp.int32, (d, d), 0)
    c = lax.broadcasted_iota(jnp.int32, (d, d), 1)
    swap = jnp.where(r == (c + d // 2) % d, 1.0, 0.0).astype(BF16)
    ones = jnp.ones((d, d), BF16)

    def rope_gains(gain_ref, scale):
        g8 = jnp.broadcast_to(gain_ref[...], (8, d))
        g_partner = pltpu.roll(g8, d // 2, 1)[0:1]
        return gain_ref[...] * cos_ref[...] * scale, g_partner * sin_ref[...] * scale

    def norm_rope(x_ref_slice, gains):
        gc, gs = gains
        xb = x_ref_slice.astype(BF16)
        x = xb.astype(F32)
        xx = x * x
        hi = xx.astype(BF16)
        lo = (xx - hi.astype(F32)).astype(BF16)
        ssq = _dot(hi, ones) + _dot(lo, ones)
        inv = lax.rsqrt(ssq * (1.0 / d) + EPS)
        return (x * gc + _dot(xb, swap) * gs) * inv

    q_gains = rope_gains(qn_ref, d ** -0.5 * LOG2E)
    k_gains = rope_gains(kn_ref, 1.0)
    for h in range(NSA_KV_HEADS):
        for g in range(NSA_GROUP):
            sl = slice((h * NSA_GROUP + g) * d, (h * NSA_GROUP + g + 1) * d)
            qt_ref[h, :, g * tm:(g + 1) * tm] = norm_rope(q_ref[:, sl], q_gains).T.astype(qt_ref.dtype)
    for h in range(NSA_KV_HEADS):
        ksl = lambda br: slice((2 * br) * NSA_KV_W + h * d, (2 * br) * NSA_KV_W + (h + 1) * d)
        vsl = lambda br: slice((2 * br + 1) * NSA_KV_W + h * d, (2 * br + 1) * NSA_KV_W + (h + 1) * d)
        cmp_ref[:, h * d:(h + 1) * d] = norm_rope(kv_ref[:, ksl(0)], k_gains).astype(cmp_ref.dtype)
        cmp_ref[:, NSA_KV_W + h * d:NSA_KV_W + (h + 1) * d] = kv_ref[:, vsl(0)].astype(cmp_ref.dtype)
        for br, k_out, vt_out in ((1, ks_ref, vst_ref), (2, kw_ref, vwt_ref)):
            k_out[:, h * d:(h + 1) * d] = norm_rope(kv_ref[:, ksl(br)], k_gains).astype(k_out.dtype)
            vt_out[h] = kv_ref[:, vsl(br)].astype(F32).T.astype(vt_out.dtype)


def nsa_prep(proj, cos, sin, qn, kn, *, offs, tm):
    m = proj.shape[0]
    assert offs["q"] == 0 and offs["kv"] == W_TOK
    d = NSA_HEAD_DIM
    row = lambda w: pl.BlockSpec((tm, w), lambda i: (i, 0))
    vt = pl.BlockSpec((NSA_KV_HEADS, d, tm), lambda i: (0, 0, i))
    return pl.pallas_call(
        functools.partial(_nsa_prep_kernel, tm=tm),
        grid=(m // tm,),
        in_specs=[pl.BlockSpec((tm, W_TOK), lambda i: (i, 0)),
                  pl.BlockSpec((tm, W_TOK), lambda i: (i, 1)),
                  row(d), row(d),
                  pl.BlockSpec((1, d), lambda i: (0, 0)),
                  pl.BlockSpec((1, d), lambda i: (0, 0))],
        out_specs=[pl.BlockSpec((None, NSA_KV_HEADS, d, NSA_GROUP * tm), lambda i: (i, 0, 0, 0)),
                   row(2 * NSA_KV_W), row(NSA_KV_W), vt, row(NSA_KV_W), vt],
        out_shape=[jax.ShapeDtypeStruct((m // tm, NSA_KV_HEADS, d, NSA_GROUP * tm), BF16),
                   jax.ShapeDtypeStruct((m, 2 * NSA_KV_W), BF16),
                   jax.ShapeDtypeStruct((m, NSA_KV_W), BF16),
                   jax.ShapeDtypeStruct((NSA_KV_HEADS, d, m), BF16),
                   jax.ShapeDtypeStruct((m, NSA_KV_W), BF16),
                   jax.ShapeDtypeStruct((NSA_KV_HEADS, d, m), BF16)],
        compiler_params=_cparams(("parallel",)),
        name="nsa_prep",
    )(proj, proj, cos, sin, qn.reshape(1, -1), kn.reshape(1, -1))


def _nsa_compress_kernel(r_ref, pos_ref, kw1_ref, kw2_ref, vw1_ref, vw2_ref, kc_ref, vct_ref):
    d = NSA_HEAD_DIM
    half = CMP_STRIDE * d
    n = r_ref.shape[0]
    posb = jnp.broadcast_to(pos_ref[...], (8, CMP_BLOCK * d)).astype(BF16)
    for off, w1_ref, w2_ref, out, transposed in ((0, kw1_ref, kw2_ref, kc_ref, False),
                                                 (NSA_KV_W, vw1_ref, vw2_ref, vct_ref, True)):
        pos_term = _dot(posb, w1_ref[...])[0:1]
        for h in range(NSA_KV_HEADS):
            cat = jnp.concatenate(
                [r_ref[:, tt * 2 * NSA_KV_W + off + h * d: tt * 2 * NSA_KV_W + off + (h + 1) * d]
                 for tt in range(CMP_STRIDE)], axis=1)
            ha = _dot(cat, w1_ref[0:half, :])
            hb = _dot(cat, w1_ref[half:2 * half, :])
            hsum = ha + pltpu.roll(hb, n - 1, 0) + pos_term
            res = _dot(_silu(hsum).astype(BF16), w2_ref[...])
            out[h] = (res.T if transposed else res).astype(out.dtype)


def nsa_compress(cmp_in, cmp_pos, kw1, kw2, vw1, vw2, *, batch, seq):
    n = seq // CMP_STRIDE
    wide = CMP_STRIDE * 2 * NSA_KV_W
    d = NSA_HEAD_DIM
    r = cmp_in.reshape(batch * n, wide)
    full = lambda shape: pl.BlockSpec(shape, lambda b: (0,) * len(shape))
    return pl.pallas_call(
        _nsa_compress_kernel,
        grid=(batch,),
        in_specs=[pl.BlockSpec((n, wide), lambda b: (b, 0)),
                  full((1, CMP_BLOCK * d)), full((CMP_BLOCK * d, d)), full((d, d)),
                  full((CMP_BLOCK * d, d)), full((d, d))],
        out_specs=[pl.BlockSpec((None, NSA_KV_HEADS, n, d), lambda b: (b, 0, 0, 0)),
                   pl.BlockSpec((None, NSA_KV_HEADS, d, n), lambda b: (b, 0, 0, 0))],
        out_shape=[jax.ShapeDtypeStruct((batch, NSA_KV_HEADS, n, d), BF16),
                   jax.ShapeDtypeStruct((batch, NSA_KV_HEADS, d, n), BF16)],
        compiler_params=_cparams(("parallel",)),
        name="nsa_compress",
    )(r, cmp_pos.reshape(1, CMP_BLOCK * d), kw1.astype(BF16), kw2.astype(BF16), vw1.astype(BF16), vw2.astype(BF16))


def _store_gated_t(o_t, g, gate_t, o_ref, branch, others=(), out_gate_ref=None):
    lane = branch * NSA_GROUP + g
    sl = slice(g * NSA_HEAD_DIM, (g + 1) * NSA_HEAD_DIM)
    gate = jax.nn.sigmoid(gate_t[lane:lane + 1, :])
    o = (o_t * gate).T
    for ref in others:
        o = o + ref[:, sl].astype(F32)
    if out_gate_ref is not None:
        o = o * _silu(out_gate_ref[:, sl].astype(F32))
    o_ref[:, sl] = o.astype(o_ref.dtype)


def _nsa_cmp_kernel(qt_ref, kc_ref, vct_ref, gl_ref, o_ref, selt_ref, *, tq, n_sel):
    i = pl.program_id(2)
    nc = kc_ref.shape[0]
    t = i * tq + lax.broadcasted_iota(jnp.int32, (1, tq), 1)
    cend = lax.broadcasted_iota(jnp.int32, (nc, 1), 0) * CMP_STRIDE + (CMP_BLOCK - 1)
    cmask = cend <= t
    any_vis = jnp.where(t >= CMP_BLOCK - 1, 1.0, 0.0)
    gate_t = gl_ref[...].astype(F32).T
    s_all = _dot(kc_ref[...], qt_ref[...])
    vct = vct_ref[...]
    psum = jnp.zeros((nc, tq), F32)
    for g in range(NSA_GROUP):
        s = jnp.where(cmask, s_all[:, g * tq:(g + 1) * tq], NEG_INF)
        e = jnp.exp2(s - jnp.max(s, axis=0, keepdims=True))
        p = e * (any_vis / jnp.sum(e, axis=0, keepdims=True))
        psum = psum + p
        _store_gated_t(_dot(vct, p.astype(BF16)), g, gate_t, o_ref, 0)

    js = lax.broadcasted_iota(jnp.int32, (n_sel, nc), 0) * SEL_BLOCK
    cs = lax.broadcasted_iota(jnp.int32, (n_sel, nc), 1) * CMP_STRIDE
    agg_t = jnp.where((cs < js + SEL_BLOCK) & (cs + CMP_BLOCK - 1 >= js), 1.0, 0.0).astype(BF16)
    hi, mid, lo = _split3(psum)
    imp = _dot(agg_t, hi) + _dot(agg_t, mid) + _dot(agg_t, lo)
    j = lax.broadcasted_iota(jnp.int32, (n_sel, 1), 0)
    bt = t // SEL_BLOCK
    forced = (j == 0) | (j == bt) | (j == bt - 1)
    sc = jnp.where(forced, SEL_BIG, jnp.where(j <= bt, imp, -SEL_BIG))
    rowi = lax.broadcasted_iota(jnp.int32, (n_sel, tq), 0).astype(F32)
    sel = jnp.zeros((n_sel, tq), F32)
    for _ in range(min(SEL_TOPK, n_sel)):
        mx = jnp.max(sc, axis=0, keepdims=True)
        idx = jnp.min(jnp.where(sc == mx, rowi, float(n_sel)), axis=0, keepdims=True)
        pick = rowi == idx
        sel = jnp.where(pick, 1.0, sel)
        sc = jnp.where(pick, -jnp.inf, sc)
    selt_ref[...] = sel


def nsa_cmp_attn(q_t, kc, vct, proj, *, batch, seq, gl_off, tq):
    m = batch * seq
    nq = seq // tq
    n_cmp = kc.shape[2]
    n_sel = seq // SEL_BLOCK
    d = NSA_HEAD_DIM
    gw = NSA_GROUP * d
    glb = gl_off // LANES
    return pl.pallas_call(
        functools.partial(_nsa_cmp_kernel, tq=tq, n_sel=n_sel),
        grid=(batch, NSA_KV_HEADS, nq),
        in_specs=[pl.BlockSpec((None, None, d, NSA_GROUP * tq), lambda b, h, i: (b * nq + i, h, 0, 0)),
                  pl.BlockSpec((None, None, n_cmp, d), lambda b, h, i: (b, h, 0, 0)),
                  pl.BlockSpec((None, None, d, n_cmp), lambda b, h, i: (b, h, 0, 0)),
                  pl.BlockSpec((tq, LANES), lambda b, h, i: (b * nq + i, glb + h))],
        out_specs=[pl.BlockSpec((tq, gw), lambda b, h, i: (b * nq + i, h)),
                   pl.BlockSpec((None, None, n_sel, tq), lambda b, h, i: (b, h, 0, i))],
        out_shape=[jax.ShapeDtypeStruct((m, W_TOK), BF16),
                   jax.ShapeDtypeStruct((batch, NSA_KV_HEADS, n_sel, seq), F32)],
        compiler_params=_cparams(("parallel", "parallel", "parallel")),
        name="nsa_cmp_attn",
    )(q_t, kc, vct, proj)


def _nsa_sel_kernel(qi_ref, ki_ref, first_ref, last_ref, qt_ref, k_ref, vt_ref, selt_ref, gl_ref, o_ref,
                    m_ref, acc_ref, *, tq, tk):
    step = pl.program_id(2)
    qi = qi_ref[step]
    ki = ki_ref[step]

    @pl.when(first_ref[step] == 1)
    def _():
        m_ref[...] = jnp.full_like(m_ref, NEG_INF)
        acc_ref[...] = jnp.zeros_like(acc_ref)

    t = qi * tq + lax.broadcasted_iota(jnp.int32, (1, tq), 1)
    kp = ki * tk + lax.broadcasted_iota(jnp.int32, (tk, 1), 0)
    per_tile = tk // SEL_BLOCK
    per_load = 8 // per_tile
    rows8 = selt_ref[pl.ds(pl.multiple_of((ki // per_load) * 8, 8), 8), :]
    sub = ki % per_load
    rows = rows8[0:per_tile]
    for u in range(1, per_load):
        rows = jnp.where(sub == u, rows8[u * per_tile:(u + 1) * per_tile], rows)
    selm = jnp.concatenate([jnp.broadcast_to(rows[u:u + 1], (SEL_BLOCK, tq)) for u in range(per_tile)], axis=0)
    mask = (selm > 0.5) & (kp <= t)
    bias = jnp.where(mask, 0.0, -jnp.inf)
    s_all = _dot(k_ref[...], qt_ref[...]) + jnp.concatenate([bias] * NSA_GROUP, axis=1)
    vt = jnp.concatenate([vt_ref[...], jnp.ones((ROWSUM_ROWS, tk), BF16)], axis=0)
    m_all = m_ref[...]
    m_rows = []
    for g in range(NSA_GROUP):
        cols = slice(g * tq, (g + 1) * tq)
        s = s_all[:, cols]
        m_old = m_all[g:g + 1, :]
        m_new = jnp.maximum(m_old, jnp.max(s, axis=0, keepdims=True))
        alpha = jnp.exp2(m_old - m_new)
        p = jnp.exp2((s - m_new).astype(BF16))
        m_rows.append(m_new)
        acc_ref[:, cols] = alpha * acc_ref[:, cols] + _dot(vt, p)
    m_ref[...] = jnp.concatenate(m_rows, axis=0)

    @pl.when(last_ref[step] == 1)
    def _():
        gate_t = gl_ref[...].astype(F32).T
        d = NSA_HEAD_DIM
        for g in range(NSA_GROUP):
            cols = slice(g * tq, (g + 1) * tq)
            o_t = acc_ref[0:d, cols] / acc_ref[d:d + 1, cols]
            _store_gated_t(o_t, g, gate_t, o_ref, 1)


def nsa_sel_attn(q_t, k, v_t, sel_t, proj, *, batch, seq, gl_off, tq, tk):
    m = batch * seq
    nq = seq // tq
    nk = seq // tk
    d = NSA_HEAD_DIM
    gw = NSA_GROUP * d
    glb = gl_off // LANES
    n_sel = seq // SEL_BLOCK
    assert 8 % (tk // SEL_BLOCK) == 0 and n_sel % 8 == 0
    steps = []
    for i in range(nq):
        hi_k = ((i + 1) * tq - 1) // tk
        for kk in range(hi_k + 1):
            steps.append((i, kk, int(kk == 0), int(kk == hi_k)))
    tabs = [jnp.asarray(np.array([s_[c] for s_ in steps], np.int32)) for c in range(4)]
    q_tile = pl.BlockSpec((tq, gw), lambda b, h, s, qi, ki, fi, la: (b * nq + qi[s], h))
    in_specs = [pl.BlockSpec((None, None, d, NSA_GROUP * tq), lambda b, h, s, qi, ki, fi, la: (b * nq + qi[s], h, 0, 0)),
                pl.BlockSpec((tk, d), lambda b, h, s, qi, ki, fi, la: (b * nk + ki[s], h)),
                pl.BlockSpec((None, d, tk), lambda b, h, s, qi, ki, fi, la: (h, 0, b * nk + ki[s])),
                pl.BlockSpec((None, None, n_sel, tq), lambda b, h, s, qi, ki, fi, la: (b, h, 0, qi[s])),
                pl.BlockSpec((tq, LANES), lambda b, h, s, qi, ki, fi, la: (b * nq + qi[s], glb + h))]
    grid_spec = pltpu.PrefetchScalarGridSpec(
        num_scalar_prefetch=4,
        grid=(batch, NSA_KV_HEADS, len(steps)),
        in_specs=in_specs,
        out_specs=q_tile,
        scratch_shapes=[pltpu.VMEM((NSA_GROUP, tq), F32),
                        pltpu.VMEM((d + ROWSUM_ROWS, NSA_GROUP * tq), F32)])
    return pl.pallas_call(
        functools.partial(_nsa_sel_kernel, tq=tq, tk=tk),
        grid_spec=grid_spec,
        out_shape=jax.ShapeDtypeStruct((m, W_TOK), BF16),
        compiler_params=_cparams(("parallel", "parallel", "arbitrary")),
        name="nsa_sel_attn",
    )(*tabs, q_t, k, v_t, sel_t, proj)


def _nsa_win_kernel(qt_ref, *rest, tq, n_tiles):
    k_refs = rest[:n_tiles]
    vt_refs = rest[n_tiles:2 * n_tiles]
    gl_ref, ocmp_ref, osel_ref, out_gate_ref, o_ref = rest[2 * n_tiles:]
    i = pl.program_id(2)
    nk = n_tiles * tq
    t = i * tq + lax.broadcasted_iota(jnp.int32, (1, tq), 1)
    kp = (i - (n_tiles - 1)) * tq + lax.broadcasted_iota(jnp.int32, (nk, 1), 0)
    mask = (kp >= 0) & (kp <= t) & (kp > t - WINDOW)
    bias = jnp.where(mask, 0.0, -jnp.inf)
    k = jnp.concatenate([r[...] for r in k_refs], axis=0)
    vt = jnp.concatenate([r[...] for r in vt_refs], axis=1)
    vt = jnp.concatenate([vt, jnp.ones((ROWSUM_ROWS, nk), BF16)], axis=0)
    s_all = _dot(k, qt_ref[...]) + jnp.concatenate([bias] * NSA_GROUP, axis=1)
    gate_t = gl_ref[...].astype(F32).T
    d = NSA_HEAD_DIM
    for g in range(NSA_GROUP):
        s = s_all[:, g * tq:(g + 1) * tq]
        p = jnp.exp2((s - jnp.max(s, axis=0, keepdims=True)).astype(BF16))
        acc = _dot(vt, p)
        _store_gated_t(acc[0:d] / acc[d:d + 1], g, gate_t, o_ref, 2, (ocmp_ref, osel_ref), out_gate_ref)


def nsa_win_attn(q_t, k, v_t, proj, o_cmp, o_sel, *, batch, seq, gl_off, gate_off, tq):
    m = batch * seq
    nq = seq // tq
    d = NSA_HEAD_DIM
    gw = NSA_GROUP * d
    glb = gl_off // LANES
    assert WINDOW % tq == 0 and gate_off % gw == 0
    n_tiles = WINDOW // tq + 1
    key_tile = lambda u: (lambda b, h, i: b * nq + jnp.maximum(i - (n_tiles - 1) + u, 0))
    q_tile = lambda col: pl.BlockSpec((tq, gw), lambda b, h, i: (b * nq + i, col + h))
    in_specs = ([pl.BlockSpec((None, None, d, NSA_GROUP * tq), lambda b, h, i: (b * nq + i, h, 0, 0))]
                + [pl.BlockSpec((tq, d), lambda b, h, i, f=key_tile(u): (f(b, h, i), h)) for u in range(n_tiles)]
                + [pl.BlockSpec((None, d, tq), lambda b, h, i, f=key_tile(u): (h, 0, f(b, h, i))) for u in range(n_tiles)]
                + [pl.BlockSpec((tq, LANES), lambda b, h, i: (b * nq + i, glb + h)),
                   q_tile(0), q_tile(0), q_tile(gate_off // gw)])
    return pl.pallas_call(
        functools.partial(_nsa_win_kernel, tq=tq, n_tiles=n_tiles),
        grid=(batch, NSA_KV_HEADS, nq),
        in_specs=in_specs,
        out_specs=q_tile(0),
        out_shape=jax.ShapeDtypeStruct((m, W_TOK), BF16),
        compiler_params=_cparams(("parallel", "parallel", "parallel")),
        name="nsa_win_attn",
    )(q_t, *([k] * n_tiles), *([v_t] * n_tiles), proj, o_cmp, o_sel, proj)


def _nsa_layout(w_in):
    kv_end = W_TOK + 6 * NSA_KV_W
    gl_end = kv_end + 3 * NSA_HEADS
    k = w_in.shape[0]
    glw = w_in[:, kv_end:gl_end].reshape(k, NSA_KV_HEADS, NSA_GROUP, 3)
    glw = glw.transpose(0, 1, 3, 2).reshape(k, NSA_KV_HEADS, 3 * NSA_GROUP)
    glw = jnp.pad(glw, ((0, 0), (0, 0), (0, LANES - 3 * NSA_GROUP))).reshape(k, NSA_KV_HEADS * LANES)
    pad_w = 2 * W_TOK - kv_end - NSA_KV_HEADS * LANES
    w = jnp.concatenate([w_in[:, :kv_end], glw, jnp.zeros((k, pad_w), w_in.dtype), w_in[:, gl_end:]], axis=1)
    offs = {"q": 0, "kv": W_TOK, "gl": kv_end, "gate": 2 * W_TOK, "memq": 3 * W_TOK}
    return w.astype(BF16), offs


def _ssd_layout(w_in):
    k = w_in.shape[0]
    xbc_end = W_TOK + W_TOK + 2 * SSM_GROUPS * SSM_STATE
    dt_end = xbc_end + SSM_HEADS
    dtw = w_in[:, xbc_end:dt_end].reshape(k, SSM_GROUPS, SSM_RANK)
    dtw = jnp.pad(dtw, ((0, 0), (0, 0), (0, LANES - SSM_RANK))).reshape(k, SSM_GROUPS * LANES)
    w = jnp.concatenate([w_in[:, :xbc_end], w_in[:, dt_end:], dtw], axis=1)
    offs = {"z": 0, "xs": W_TOK, "B": 2 * W_TOK, "C": 2 * W_TOK + SSM_GROUPS * SSM_STATE,
            "memq": xbc_end, "dt": xbc_end + 2 * W_MEM}
    return w.astype(BF16), offs


def nsa_mixer(proj, positions, qn, kn, cmp_pos, kw1, kw2, vw1, vw2, *, batch, seq, offs):
    cos, sin = rope_tables(positions, tm=min(1024, batch * seq))
    tq = 256
    common = dict(batch=batch, seq=seq, gl_off=offs["gl"])
    q_t, cmp_in, k_s, vt_s, k_w, vt_w = nsa_prep(proj, cos, sin, qn, kn, offs=offs, tm=tq)
    kc, vct = nsa_compress(cmp_in, cmp_pos, kw1, kw2, vw1, vw2, batch=batch, seq=seq)
    o_cmp, sel_t = nsa_cmp_attn(q_t, kc, vct, proj, tq=tq, **common)
    o_sel = nsa_sel_attn(q_t, k_s, vt_s, sel_t, proj, tq=tq, tk=min(512, seq), **common)
    return nsa_win_attn(q_t, k_w, vt_w, proj, o_cmp, o_sel, gate_off=offs["gate"], tq=tq, **common)


def _layer(x2, mem2, mem_norm, norm, w_in_b, w_out, mem_wkv, mem_qn, mem_kn, mixer, memq_off, *, seq, gnorm=None):
    m = x2.shape[0]
    proj = norm_matmul(x2, norm, w_in_b, tm=_tile(m, 1024), tn=_tile(w_in_b.shape[1], 1024), out_dtype=BF16)
    kv = norm_matmul(mem2, mem_norm, mem_wkv.astype(BF16), tm=_tile(mem2.shape[0], 512), tn=512, out_dtype=F32)
    y_mem = mem_attn(proj, kv, mem_qn, mem_kn, q_off=memq_off, seq=seq, tq=min(512, seq))
    y_tok = mixer(proj)
    w_tok = w_out[:W_TOK].astype(BF16)
    w_mem = w_out[W_TOK:].astype(BF16)
    norm_tok = gnorm is not None
    gain = gnorm if norm_tok else jnp.ones((W_TOK,), F32)
    tm, tn = (_tile(m, 512), 1024) if norm_tok else (_tile(m, 1024), 512)
    return out_proj(y_tok, y_mem, w_tok, w_mem, x2, gain, norm_tok=norm_tok, tm=tm, tn=tn)


def kernel(x, mem, positions, mem_norm, l0_norm, l0_w_in, l0_w_out, l0_mem_wkv, l0_mem_qnorm, l0_mem_knorm, l0_pool_w, l0_pool_scale, l1_norm, l1_w_in, l1_w_out, l1_mem_wkv, l1_mem_qnorm, l1_mem_knorm, l1_qnorm, l1_knorm, l1_cmp_pos, l1_cmp_k_w1, l1_cmp_k_w2, l1_cmp_v_w1, l1_cmp_v_w2, l2_norm, l2_w_in, l2_w_out, l2_mem_wkv, l2_mem_qnorm, l2_mem_knorm, l2_conv_w, l2_conv_b, l2_dt_bias, l2_A_log, l2_D, l2_gnorm, l3_norm, l3_w_in, l3_w_out, l3_mem_wkv, l3_mem_qnorm, l3_mem_knorm, l3_v_norm, l3_sgu_w, l3_sgu_b):
    batch, seq, d = x.shape
    x2 = x.reshape(batch * seq, d)
    mem2 = mem.reshape(batch * MEM_TOKENS, d)

    x2 = _layer(x2, mem2, mem_norm, l0_norm, l0_w_in.astype(BF16), l0_w_out, l0_mem_wkv, l0_mem_qnorm, l0_mem_knorm,
                lambda p: pool_mixer(p, l0_pool_w, l0_pool_scale, seq=seq, tm=256), 2 * W_TOK, seq=seq)

    w1, offs1 = _nsa_layout(l1_w_in)
    x2 = _layer(x2, mem2, mem_norm, l1_norm, w1, l1_w_out, l1_mem_wkv, l1_mem_qnorm, l1_mem_knorm,
                lambda p: nsa_mixer(p, positions, l1_qnorm, l1_knorm, l1_cmp_pos, l1_cmp_k_w1, l1_cmp_k_w2,
                                    l1_cmp_v_w1, l1_cmp_v_w2, batch=batch, seq=seq, offs=offs1),
                offs1["memq"], seq=seq)

    w2, offs2 = _ssd_layout(l2_w_in)
    x2 = _layer(x2, mem2, mem_norm, l2_norm, w2, l2_w_out, l2_mem_wkv, l2_mem_qnorm, l2_mem_knorm,
                lambda p: ssd_mixer(p, l2_conv_w, l2_conv_b, l2_dt_bias, l2_A_log, l2_D, batch=batch, seq=seq, offs=offs2),
                offs2["memq"], seq=seq, gnorm=l2_gnorm)

    x2 = _layer(x2, mem2, mem_norm, l3_norm, l3_w_in.astype(BF16), l3_w_out, l3_mem_wkv, l3_mem_qnorm, l3_mem_knorm,
                lambda p: sgu_mixer(p, l3_v_norm, l3_sgu_w, l3_sgu_b, tm=256), 3 * W_TOK, seq=seq)
    return x2.reshape(batch, seq, d)
```

```python
import functools
import math

import numpy as np
import jax
import jax.numpy as jnp
from jax import lax
from jax.experimental import pallas as pl
from jax.experimental.pallas import tpu as pltpu

F32 = jnp.float32
BF16 = jnp.bfloat16

D_MODEL = 2048
EPS = 1e-6
ROPE_THETA = 10000.0
NEG_INF = -1e30
SEL_BIG = 1e9
LOG2E = 1.0 / math.log(2.0)

W_TOK = 2 * D_MODEL
MEM_TOKENS = 256
MEM_HEADS = 4
MEM_HEAD_DIM = D_MODEL // 8
W_MEM = MEM_HEADS * MEM_HEAD_DIM

POOL_GROUPS = 4
POOL_GW = W_TOK // POOL_GROUPS
POOL_HALO = 16

NSA_HEADS = 32
NSA_KV_HEADS = 4
NSA_GROUP = NSA_HEADS // NSA_KV_HEADS
NSA_HEAD_DIM = 128
NSA_KV_W = NSA_KV_HEADS * NSA_HEAD_DIM
CMP_BLOCK = 32
CMP_STRIDE = 16
SEL_BLOCK = 64
SEL_TOPK = 16
WINDOW = 512
ROWSUM_ROWS = 16

SSM_HEAD_DIM = 64
SSM_HEADS = W_TOK // SSM_HEAD_DIM
SSM_GROUPS = 8
SSM_RANK = SSM_HEADS // SSM_GROUPS
SSM_STATE = 128
CONV_W = 4
SSD_CHUNK = 256
SSM_GX = W_TOK // SSM_GROUPS
SSD_HALO = 16

SGU_CHUNK = 128
SGU_GROUPS = 8
SGU_GW = W_TOK // SGU_GROUPS

LANES = 128
VMEM_LIMIT = 56 * 1024 * 1024


def _cparams(sem):
    return pltpu.CompilerParams(dimension_semantics=sem, vmem_limit_bytes=VMEM_LIMIT)


def _tile(n, pref):
    t = pref
    while n % t:
        t //= 2
    return t


def _silu(x):
    return x * jax.nn.sigmoid(x)


def _gelu_exact(x):
    return 0.5 * x * (1.0 + lax.erf(x * (1.0 / math.sqrt(2.0))))


def _dot(a, b):
    return jnp.dot(a, b, preferred_element_type=F32)


def _dot_nt(a, b):
    return lax.dot_general(a, b, (((1,), (1,)), ((), ())), preferred_element_type=F32)


def _split3(x):
    hi = x.astype(BF16)
    r1 = x - hi.astype(F32)
    mid = r1.astype(BF16)
    lo = (r1 - mid.astype(F32)).astype(BF16)
    return hi, mid, lo


def _norm_matmul_kernel(x_ref, g_ref, w_ref, o_ref, h_ref):
    @pl.when(pl.program_id(1) == 0)
    def _():
        x = x_ref[...].astype(F32)
        ms = jnp.mean(x * x, axis=-1, keepdims=True)
        h_ref[...] = (x * lax.rsqrt(ms + EPS) * g_ref[...]).astype(BF16)

    o_ref[...] = _dot(h_ref[...], w_ref[...]).astype(o_ref.dtype)


def norm_matmul(x, g, w, *, tm, tn, out_dtype):
    m, k = x.shape
    n = w.shape[1]
    return pl.pallas_call(
        _norm_matmul_kernel,
        grid=(m // tm, n // tn),
        in_specs=[pl.BlockSpec((tm, k), lambda i, j: (i, 0)),
                  pl.BlockSpec((1, k), lambda i, j: (0, 0)),
                  pl.BlockSpec((k, tn), lambda i, j: (0, j))],
        out_specs=pl.BlockSpec((tm, tn), lambda i, j: (i, j)),
        out_shape=jax.ShapeDtypeStruct((m, n), out_dtype),
        scratch_shapes=[pltpu.VMEM((tm, k), BF16)],
        compiler_params=_cparams(("parallel", "arbitrary")),
        name="norm_matmul",
    )(x, g.reshape(1, k), w)


def _out_proj_kernel(yt_ref, ym_ref, wt_ref, wm_ref, x_ref, g_ref, o_ref, *scratch, norm_tok):
    if norm_tok:
        (h_ref,) = scratch

        @pl.when(pl.program_id(1) == 0)
        def _():
            y = yt_ref[...].astype(F32)
            ms = jnp.mean(y * y, axis=-1, keepdims=True)
            h_ref[...] = (y * lax.rsqrt(ms + EPS) * g_ref[...]).astype(BF16)

        yt = h_ref[...]
    else:
        yt = yt_ref[...]
    acc = _dot(yt, wt_ref[...]) + _dot(ym_ref[...], wm_ref[...])
    o_ref[...] = x_ref[...] + acc


def out_proj(y_tok, y_mem, w_tok, w_mem, x, gain, *, norm_tok, tm, tn):
    m, kt = y_tok.shape
    km = y_mem.shape[1]
    n = w_tok.shape[1]
    return pl.pallas_call(
        functools.partial(_out_proj_kernel, norm_tok=norm_tok),
        grid=(m // tm, n // tn),
        in_specs=[pl.BlockSpec((tm, kt), lambda i, j: (i, 0)),
                  pl.BlockSpec((tm, km), lambda i, j: (i, 0)),
                  pl.BlockSpec((kt, tn), lambda i, j: (0, j)),
                  pl.BlockSpec((km, tn), lambda i, j: (0, j)),
                  pl.BlockSpec((tm, tn), lambda i, j: (i, j)),
                  pl.BlockSpec((1, kt), lambda i, j: (0, 0))],
        out_specs=pl.BlockSpec((tm, tn), lambda i, j: (i, j)),
        out_shape=jax.ShapeDtypeStruct((m, n), F32),
        scratch_shapes=[pltpu.VMEM((tm, kt), BF16)] if norm_tok else [],
        compiler_params=_cparams(("parallel", "arbitrary")),
        name="out_proj",
    )(y_tok, y_mem, w_tok, w_mem, x, gain.reshape(1, kt))


def _mem_attn_kernel(q_ref, gate_ref, kv_ref, qn_ref, kn_ref, o_ref):
    hd = MEM_HEAD_DIM
    scale = hd ** -0.5
    for h in range(MEM_HEADS):
        q = q_ref[:, h * hd:(h + 1) * hd].astype(F32)
        q = q * lax.rsqrt(jnp.mean(q * q, axis=-1, keepdims=True) + EPS) * qn_ref[...]
        k = kv_ref[:, h * hd:(h + 1) * hd].astype(F32)
        k = k * lax.rsqrt(jnp.mean(k * k, axis=-1, keepdims=True) + EPS) * kn_ref[...]
        v = kv_ref[:, W_MEM + h * hd:W_MEM + (h + 1) * hd].astype(BF16)
        s = _dot_nt((q * scale).astype(BF16), k.astype(BF16))
        e = jnp.exp(s - jnp.max(s, axis=-1, keepdims=True))
        o = _dot(e.astype(BF16), v) / jnp.sum(e, axis=-1, keepdims=True)
        gate = gate_ref[:, h * hd:(h + 1) * hd].astype(F32)
        o_ref[:, h * hd:(h + 1) * hd] = (o * _silu(gate)).astype(o_ref.dtype)


def mem_attn(proj, kv, qn, kn, *, q_off, seq, tq):
    m = proj.shape[0]
    nq = seq // tq
    qb = q_off // W_MEM
    return pl.pallas_call(
        _mem_attn_kernel,
        grid=(m // tq,),
        in_specs=[pl.BlockSpec((tq, W_MEM), lambda i: (i, qb)),
                  pl.BlockSpec((tq, W_MEM), lambda i: (i, qb + 1)),
                  pl.BlockSpec((MEM_TOKENS, 2 * W_MEM), lambda i: (i // nq, 0)),
                  pl.BlockSpec((1, MEM_HEAD_DIM), lambda i: (0, 0)),
                  pl.BlockSpec((1, MEM_HEAD_DIM), lambda i: (0, 0))],
        out_specs=pl.BlockSpec((tq, W_MEM), lambda i: (i, 0)),
        out_shape=jax.ShapeDtypeStruct((m, W_MEM), BF16),
        compiler_params=_cparams(("parallel",)),
        name="mem_attn",
    )(proj, proj, kv, qn.reshape(1, -1), kn.reshape(1, -1))


def _pool_kernel(v_ref, halo_ref, gate_ref, w_ref, scale_ref, o_ref, *, tm, seq):
    g = pl.program_id(0)
    i = pl.program_id(1)
    win = jnp.left_shift(2, g)
    t_seq = lax.rem(i * tm, seq)
    r = lax.broadcasted_iota(jnp.int32, (tm, 1), 0)
    c = lax.broadcasted_iota(jnp.int32, (1, tm), 1)
    a_main = jnp.where((c <= r) & (c > r - win), 1.0, 0.0).astype(BF16)
    ch = lax.broadcasted_iota(jnp.int32, (1, POOL_HALO), 1)
    a_halo = jnp.where((ch > r + POOL_HALO - win) & (t_seq > 0), 1.0, 0.0).astype(BF16)
    v = v_ref[...]
    wsum = _dot(a_main, v.astype(BF16)) + _dot(a_halo, halo_ref[...].astype(BF16))
    cnt = jnp.minimum(t_seq + r + 1, win).astype(F32)
    mix = wsum / cnt - v.astype(F32)
    out = _dot(mix.astype(BF16), w_ref[...]) * scale_ref[...]
    o_ref[...] = (out * _silu(gate_ref[...].astype(F32))).astype(o_ref.dtype)


def pool_mixer(proj, pool_w, pool_scale, *, seq, tm):
    m = proj.shape[0]
    hb = tm // POOL_HALO
    return pl.pallas_call(
        functools.partial(_pool_kernel, tm=tm, seq=seq),
        grid=(POOL_GROUPS, m // tm),
        in_specs=[pl.BlockSpec((tm, POOL_GW), lambda g, i: (i, g)),
                  pl.BlockSpec((POOL_HALO, POOL_GW), lambda g, i: (jnp.maximum(i * hb - 1, 0), g)),
                  pl.BlockSpec((tm, POOL_GW), lambda g, i: (i, POOL_GROUPS + g)),
                  pl.BlockSpec((None, POOL_GW, POOL_GW), lambda g, i: (g, 0, 0)),
                  pl.BlockSpec((1, POOL_GW), lambda g, i: (0, g))],
        out_specs=pl.BlockSpec((tm, POOL_GW), lambda g, i: (i, g)),
        out_shape=jax.ShapeDtypeStruct((m, W_TOK), BF16),
        compiler_params=_cparams(("parallel", "parallel")),
        name="pool_mixer",
    )(proj, proj, proj, pool_w.astype(BF16), pool_scale.reshape(1, W_TOK))


def _sgu_kernel(u_ref, v_ref, gate_ref, vn_ref, w_ref, bt_ref, o_ref, vs_ref, *, tm):
    ssq = jnp.zeros((tm, 1), F32)
    for g in range(SGU_GROUPS):
        sl = slice(g * SGU_GW, (g + 1) * SGU_GW)
        vg = _gelu_exact(v_ref[:, sl].astype(F32))
        vs_ref[:, sl] = vg
        ssq = ssq + jnp.sum(vg * vg, axis=-1, keepdims=True)
    inv = lax.rsqrt(ssq * (1.0 / W_TOK) + EPS)
    r = lax.broadcasted_iota(jnp.int32, (SGU_CHUNK, SGU_CHUNK), 0)
    c = lax.broadcasted_iota(jnp.int32, (SGU_CHUNK, SGU_CHUNK), 1)
    for g in range(SGU_GROUPS):
        sl = slice(g * SGU_GW, (g + 1) * SGU_GW)
        w = jnp.where(r >= c, w_ref[g], 0.0).astype(BF16)
        vn = (vs_ref[:, sl] * inv * vn_ref[:, sl]).astype(BF16)
        for ck in range(tm // SGU_CHUNK):
            rs = slice(ck * SGU_CHUNK, (ck + 1) * SGU_CHUNK)
            mixed = _dot(w, vn[rs]) + bt_ref[:, g:g + 1]
            u = _gelu_exact(u_ref[rs, sl].astype(F32))
            o_ref[rs, sl] = (u * mixed * _silu(gate_ref[rs, sl].astype(F32))).astype(o_ref.dtype)


def sgu_mixer(proj, v_norm, sgu_w, sgu_b, *, tm):
    m = proj.shape[0]
    return pl.pallas_call(
        functools.partial(_sgu_kernel, tm=tm),
        grid=(m // tm,),
        in_specs=[pl.BlockSpec((tm, W_TOK), lambda i: (i, 0)),
                  pl.BlockSpec((tm, W_TOK), lambda i: (i, 1)),
                  pl.BlockSpec((tm, W_TOK), lambda i: (i, 2)),
                  pl.BlockSpec((1, W_TOK), lambda i: (0, 0)),
                  pl.BlockSpec((SGU_GROUPS, SGU_CHUNK, SGU_CHUNK), lambda i: (0, 0, 0)),
                  pl.BlockSpec((SGU_CHUNK, SGU_GROUPS), lambda i: (0, 0))],
        out_specs=pl.BlockSpec((tm, W_TOK), lambda i: (i, 0)),
        out_shape=jax.ShapeDtypeStruct((m, W_TOK), BF16),
        scratch_shapes=[pltpu.VMEM((tm, W_TOK), F32)],
        compiler_params=_cparams(("parallel",)),
        name="sgu_mixer",
    )(proj, proj, proj, v_norm.reshape(1, W_TOK), sgu_w, sgu_b.T)


def _conv_silu(raw, prev, w, b):
    L = raw.shape[0]
    rr = lax.broadcasted_iota(jnp.int32, ((CONV_W - 1) * L, 1), 0)
    src = rr % L - (rr // L + 1)
    main = jnp.where(lax.broadcasted_iota(jnp.int32, (1, L), 1) == src, 1.0, 0.0).astype(BF16)
    shifted = _dot(main, raw)
    row8 = lax.broadcasted_iota(jnp.int32, (8, 1), 0)
    prev32 = prev.astype(F32)
    acc = raw.astype(F32) * w[CONV_W - 1:CONV_W, :] + b
    for k in range(1, CONV_W):
        blk = shifted[(k - 1) * L:k * L]
        head = blk[0:8] + jnp.where(row8 < k, pltpu.roll(prev32, k, 0)[0:8], 0.0)
        blk = jnp.concatenate([head, blk[8:]], axis=0)
        acc = acc + blk * w[CONV_W - 1 - k:CONV_W - k, :]
    return _silu(acc)


def _expand_heads(cols, n_rows):
    lane_head = lax.broadcasted_iota(jnp.int32, (1, SSM_GX), 1) // SSM_HEAD_DIM
    out = jnp.zeros((n_rows, SSM_GX), F32)
    for r in range(SSM_RANK):
        out = jnp.where(lane_head == r, cols[:, r:r + 1], out)
    return out


def _ssd_kernel(x_ref, b_ref, c_ref, dt_ref, z_ref, cwx_ref, cwb_ref, cwc_ref, cbx_ref, cbb_ref, cbc_ref,
                dtb_ref, alog_ref, dexp_ref, o_ref, state_ref, prev_ref):
    L = SSD_CHUNK
    ck = pl.program_id(2)

    @pl.when(ck == 0)
    def _():
        state_ref[...] = jnp.zeros_like(state_ref)
        prev_ref[...] = jnp.zeros_like(prev_ref)

    raw = jnp.concatenate([x_ref[...], b_ref[...], c_ref[...]], axis=1).astype(BF16)
    conv_w = jnp.concatenate([cwx_ref[...], cwb_ref[...], cwc_ref[...]], axis=1)
    conv_b = jnp.concatenate([cbx_ref[...], cbb_ref[...], cbc_ref[...]], axis=1)
    conv = _conv_silu(raw, prev_ref[...], conv_w, conv_b)
    prev_ref[...] = raw[L - SSD_HALO:L]
    x = conv[:, :SSM_GX]
    bm = conv[:, SSM_GX:SSM_GX + SSM_STATE]
    cm = conv[:, SSM_GX + SSM_STATE:]

    dt = jax.nn.softplus(dt_ref[...].astype(F32) + dtb_ref[...])
    lane = lax.broadcasted_iota(jnp.int32, (1, LANES), 1)
    dt = jnp.where(lane < SSM_RANK, dt, 0.0)
    dta = dt * (-jnp.exp(alog_ref[...]) * LOG2E)
    r = lax.broadcasted_iota(jnp.int32, (L, L), 0)
    c = lax.broadcasted_iota(jnp.int32, (L, L), 1)
    tri = r >= c
    tril = jnp.where(tri, 1.0, 0.0).astype(BF16)
    hi, mid, lo = _split3(dta)
    a_col = _dot(tril, hi) + _dot(tril, mid) + _dot(tril, lo)
    a_row = a_col.T
    a_end = a_col[L - 1:L, :]

    stacked = jnp.concatenate([dt, jnp.exp2(a_col), jnp.exp2(a_end - a_col)], axis=0)
    head_of_row = lax.broadcasted_iota(jnp.int32, (LANES, 1), 0)
    head_of_lane = lax.broadcasted_iota(jnp.int32, (1, SSM_GX), 1) // SSM_HEAD_DIM
    spread = jnp.where(head_of_row == head_of_lane, 1.0, 0.0).astype(BF16)
    hi = stacked.astype(BF16)
    lo = (stacked - hi.astype(F32)).astype(BF16)
    wide = _dot(hi, spread) + _dot(lo, spread)
    xdt = x * wide[0:L]
    xdt_b = xdt.astype(BF16)

    cb = jnp.where(tri, _dot_nt(cm.astype(BF16), bm.astype(BF16)), 0.0)
    lane_lo = lax.broadcasted_iota(jnp.int32, (1, LANES), 1) < SSM_HEAD_DIM
    y_parts = []
    for j in range(SSM_RANK // 2):
        xp = xdt_b[:, j * LANES:(j + 1) * LANES]
        ys = []
        for r_ in (2 * j, 2 * j + 1):
            diff = a_col[:, r_:r_ + 1] - a_row[r_:r_ + 1, :]
            mm = cb * jnp.exp2(jnp.minimum(diff, 0.0))
            ys.append(_dot(mm.astype(BF16), xp))
        y_parts.append(jnp.where(lane_lo, ys[0], ys[1]))
    y_in = jnp.concatenate(y_parts, axis=1)

    state = state_ref[...]
    y_st = _dot(cm.astype(BF16), state.astype(BF16)) * wide[L:2 * L]
    xw = (xdt * wide[2 * L:3 * L]).astype(BF16)
    state_ref[...] = state * _expand_heads(jnp.exp2(a_end), 1) + _dot(bm.T.astype(BF16), xw)

    y = y_in + y_st + dexp_ref[...] * x
    o_ref[...] = (y * _silu(z_ref[...].astype(F32))).astype(o_ref.dtype)


def ssd_mixer(proj, conv_w, conv_b, dt_bias, a_log, d_skip, *, batch, seq, offs):
    m = proj.shape[0]
    L = SSD_CHUNK
    nc = seq // L
    G = SSM_GROUPS
    xo, bo, co, dto = (offs[k] for k in ("xs", "B", "C", "dt"))

    def pad_heads(p):
        return jnp.pad(p.reshape(G, 1, SSM_RANK), ((0, 0), (0, 0), (0, LANES - SSM_RANK)))

    d_exp = jnp.repeat(d_skip, SSM_HEAD_DIM).reshape(G, 1, SSM_GX)
    cb2 = conv_b.reshape(1, -1)
    row = lambda b, g, c: b * nc + c
    return pl.pallas_call(
        _ssd_kernel,
        grid=(batch, G, nc),
        in_specs=[pl.BlockSpec((L, SSM_GX), lambda b, g, c: (row(b, g, c), xo // SSM_GX + g)),
                  pl.BlockSpec((L, LANES), lambda b, g, c: (row(b, g, c), bo // LANES + g)),
                  pl.BlockSpec((L, LANES), lambda b, g, c: (row(b, g, c), co // LANES + g)),
                  pl.BlockSpec((L, LANES), lambda b, g, c: (row(b, g, c), dto // LANES + g)),
                  pl.BlockSpec((L, SSM_GX), lambda b, g, c: (row(b, g, c), g)),
                  pl.BlockSpec((CONV_W, SSM_GX), lambda b, g, c: (0, g)),
                  pl.BlockSpec((CONV_W, LANES), lambda b, g, c: (0, W_TOK // LANES + g)),
                  pl.BlockSpec((CONV_W, LANES), lambda b, g, c: (0, (W_TOK + G * SSM_STATE) // LANES + g)),
                  pl.BlockSpec((1, SSM_GX), lambda b, g, c: (0, g)),
                  pl.BlockSpec((1, LANES), lambda b, g, c: (0, W_TOK // LANES + g)),
                  pl.BlockSpec((1, LANES), lambda b, g, c: (0, (W_TOK + G * SSM_STATE) // LANES + g)),
                  pl.BlockSpec((None, 1, LANES), lambda b, g, c: (g, 0, 0)),
                  pl.BlockSpec((None, 1, LANES), lambda b, g, c: (g, 0, 0)),
                  pl.BlockSpec((None, 1, SSM_GX), lambda b, g, c: (g, 0, 0))],
        out_specs=pl.BlockSpec((L, SSM_GX), lambda b, g, c: (row(b, g, c), g)),
        out_shape=jax.ShapeDtypeStruct((m, W_TOK), BF16),
        scratch_shapes=[pltpu.VMEM((SSM_STATE, SSM_GX), F32),
                        pltpu.VMEM((SSD_HALO, SSM_GX + 2 * SSM_STATE), BF16)],
        compiler_params=_cparams(("parallel", "parallel", "arbitrary")),
        name="ssd_mixer",
    )(proj, proj, proj, proj, proj, conv_w, conv_w, conv_w, cb2, cb2, cb2,
      pad_heads(dt_bias), pad_heads(a_log), d_exp)


def _rope_table_kernel(pos_ref, inv_ref, cos_ref, sin_ref):
    ang = pos_ref[...].astype(F32) * inv_ref[...]
    lane = lax.broadcasted_iota(jnp.int32, (1, NSA_HEAD_DIM), 1)
    cos_ref[...] = jnp.cos(ang)
    sin_ref[...] = jnp.where(lane < NSA_HEAD_DIM // 2, -1.0, 1.0) * jnp.sin(ang)


def rope_tables(positions, *, tm):
    m = positions.size
    inv = ROPE_THETA ** (-jnp.arange(0, NSA_HEAD_DIM, 2, dtype=F32) / NSA_HEAD_DIM)
    inv2 = jnp.concatenate([inv, inv]).reshape(1, NSA_HEAD_DIM)
    return pl.pallas_call(
        _rope_table_kernel,
        grid=(m // tm,),
        in_specs=[pl.BlockSpec((tm, 1), lambda i: (i, 0)),
                  pl.BlockSpec((1, NSA_HEAD_DIM), lambda i: (0, 0))],
        out_specs=[pl.BlockSpec((tm, NSA_HEAD_DIM), lambda i: (i, 0))] * 2,
        out_shape=[jax.ShapeDtypeStruct((m, NSA_HEAD_DIM), F32)] * 2,
        compiler_params=_cparams(("parallel",)),
        name="rope_tables",
    )(positions.reshape(m, 1), inv2)


def _nsa_prep_kernel(q_ref, kv_ref, cos_ref, sin_ref, qn_ref, kn_ref,
                     qt_ref, cmp_ref, ks_ref, vst_ref, kw_ref, vwt_ref, *, tm):
    d = NSA_HEAD_DIM
    r = lax.broadcasted_iota(jnp.int32, (d, d), 0)
    c = lax.broadcasted_iota(jnp.int32, (d, d), 1)
    swap = jnp.where(r == (c + d // 2) % d, 1.0, 0.0).astype(BF16)
    ones = jnp.ones((d, d), BF16)

    def rope_gains(gain_ref, scale):
        g8 = jnp.broadcast_to(gain_ref[...], (8, d))
        g_partner = pltpu.roll(g8, d // 2, 1)[0:1]
        return gain_ref[...] * cos_ref[...] * scale, g_partner * sin_ref[...] * scale

    def norm_rope(x_ref_slice, gains):
        gc, gs = gains
        xb = x_ref_slice.astype(BF16)
        x = xb.astype(F32)
        xx = x * x
        hi = xx.astype(BF16)
        lo = (xx - hi.astype(F32)).astype(BF16)
        ssq = _dot(hi, ones) + _dot(lo, ones)
        inv = lax.rsqrt(ssq * (1.0 / d) + EPS)
        return (x * gc + _dot(xb, swap) * gs) * inv

    q_gains = rope_gains(qn_ref, d ** -0.5 * LOG2E)
    k_gains = rope_gains(kn_ref, 1.0)
    for h in range(NSA_KV_HEADS):
        for g in range(NSA_GROUP):
            sl = slice((h * NSA_GROUP + g) * d, (h * NSA_GROUP + g + 1) * d)
            qt_ref[h, :, g * tm:(g + 1) * tm] = norm_rope(q_ref[:, sl], q_gains).T.astype(qt_ref.dtype)
    for h in range(NSA_KV_HEADS):
        ksl = lambda br: slice((2 * br) * NSA_KV_W + h * d, (2 * br) * NSA_KV_W + (h + 1) * d)
        vsl = lambda br: slice((2 * br + 1) * NSA_KV_W + h * d, (2 * br + 1) * NSA_KV_W + (h + 1) * d)
        cmp_ref[:, h * d:(h + 1) * d] = norm_rope(kv_ref[:, ksl(0)], k_gains).astype(cmp_ref.dtype)
        cmp_ref[:, NSA_KV_W + h * d:NSA_KV_W + (h + 1) * d] = kv_ref[:, vsl(0)].astype(cmp_ref.dtype)
        for br, k_out, vt_out in ((1, ks_ref, vst_ref), (2, kw_ref, vwt_ref)):
            k_out[:, h * d:(h + 1) * d] = norm_rope(kv_ref[:, ksl(br)], k_gains).astype(k_out.dtype)
            vt_out[h] = kv_ref[:, vsl(br)].astype(F32).T.astype(vt_out.dtype)


def nsa_prep(proj, cos, sin, qn, kn, *, offs, tm):
    m = proj.shape[0]
    assert offs["q"] == 0 and offs["kv"] == W_TOK
    d = NSA_HEAD_DIM
    row = lambda w: pl.BlockSpec((tm, w), lambda i: (i, 0))
    vt = pl.BlockSpec((NSA_KV_HEADS, d, tm), lambda i: (0, 0, i))
    return pl.pallas_call(
        functools.partial(_nsa_prep_kernel, tm=tm),
        grid=(m // tm,),
        in_specs=[pl.BlockSpec((tm, W_TOK), lambda i: (i, 0)),
                  pl.BlockSpec((tm, W_TOK), lambda i: (i, 1)),
                  row(d), row(d),
                  pl.BlockSpec((1, d), lambda i: (0, 0)),
                  pl.BlockSpec((1, d), lambda i: (0, 0))],
        out_specs=[pl.BlockSpec((None, NSA_KV_HEADS, d, NSA_GROUP * tm), lambda i: (i, 0, 0, 0)),
                   row(2 * NSA_KV_W), row(NSA_KV_W), vt, row(NSA_KV_W), vt],
        out_shape=[jax.ShapeDtypeStruct((m // tm, NSA_KV_HEADS, d, NSA_GROUP * tm), BF16),
                   jax.ShapeDtypeStruct((m, 2 * NSA_KV_W), BF16),
                   jax.ShapeDtypeStruct((m, NSA_KV_W), BF16),
                   jax.ShapeDtypeStruct((NSA_KV_HEADS, d, m), BF16),
                   jax.ShapeDtypeStruct((m, NSA_KV_W), BF16),
                   jax.ShapeDtypeStruct((NSA_KV_HEADS, d, m), BF16)],
        compiler_params=_cparams(("parallel",)),
        name="nsa_prep",
    )(proj, proj, cos, sin, qn.reshape(1, -1), kn.reshape(1, -1))


def _nsa_compress_kernel(r_ref, pos_ref, kw1_ref, kw2_ref, vw1_ref, vw2_ref, kc_ref, vct_ref):
    d = NSA_HEAD_DIM
    half = CMP_STRIDE * d
    n = r_ref.shape[0]
    posb = jnp.broadcast_to(pos_ref[...], (8, CMP_BLOCK * d)).astype(BF16)
    for off, w1_ref, w2_ref, out, transposed in ((0, kw1_ref, kw2_ref, kc_ref, False),
                                                 (NSA_KV_W, vw1_ref, vw2_ref, vct_ref, True)):
        pos_term = _dot(posb, w1_ref[...])[0:1]
        for h in range(NSA_KV_HEADS):
            cat = jnp.concatenate(
                [r_ref[:, tt * 2 * NSA_KV_W + off + h * d: tt * 2 * NSA_KV_W + off + (h + 1) * d]
                 for tt in range(CMP_STRIDE)], axis=1)
            ha = _dot(cat, w1_ref[0:half, :])
            hb = _dot(cat, w1_ref[half:2 * half, :])
            hsum = ha + pltpu.roll(hb, n - 1, 0) + pos_term
            res = _dot(_silu(hsum).astype(BF16), w2_ref[...])
            out[h] = (res.T if transposed else res).astype(out.dtype)


def nsa_compress(cmp_in, cmp_pos, kw1, kw2, vw1, vw2, *, batch, seq):
    n = seq // CMP_STRIDE
    wide = CMP_STRIDE * 2 * NSA_KV_W
    d = NSA_HEAD_DIM
    r = cmp_in.reshape(batch * n, wide)
    full = lambda shape: pl.BlockSpec(shape, lambda b: (0,) * len(shape))
    return pl.pallas_call(
        _nsa_compress_kernel,
        grid=(batch,),
        in_specs=[pl.BlockSpec((n, wide), lambda b: (b, 0)),
                  full((1, CMP_BLOCK * d)), full((CMP_BLOCK * d, d)), full((d, d)),
                  full((CMP_BLOCK * d, d)), full((d, d))],
        out_specs=[pl.BlockSpec((None, NSA_KV_HEADS, n, d), lambda b: (b, 0, 0, 0)),
                   pl.BlockSpec((None, NSA_KV_HEADS, d, n), lambda b: (b, 0, 0, 0))],
        out_shape=[jax.ShapeDtypeStruct((batch, NSA_KV_HEADS, n, d), BF16),
                   jax.ShapeDtypeStruct((batch, NSA_KV_HEADS, d, n), BF16)],
        compiler_params=_cparams(("parallel",)),
        name="nsa_compress",
    )(r, cmp_pos.reshape(1, CMP_BLOCK * d), kw1.astype(BF16), kw2.astype(BF16), vw1.astype(BF16), vw2.astype(BF16))


def _store_gated_t(o_t, g, gate_t, o_ref, branch, others=(), out_gate_ref=None, rows=slice(None)):
    lane = branch * NSA_GROUP + g
    sl = slice(g * NSA_HEAD_DIM, (g + 1) * NSA_HEAD_DIM)
    gate = jax.nn.sigmoid(gate_t[lane:lane + 1, :])
    o = (o_t * gate).T
    for ref in others:
        o = o + ref[rows, sl].astype(F32)
    if out_gate_ref is not None:
        o = o * _silu(out_gate_ref[rows, sl].astype(F32))
    o_ref[rows, sl] = o.astype(o_ref.dtype)


def _nsa_cmp_kernel(qt_ref, kc_ref, vct_ref, gl_ref, o_ref, selt_ref, *, tq, n_sel):
    i = pl.program_id(2)
    nc = kc_ref.shape[0]
    t = i * tq + lax.broadcasted_iota(jnp.int32, (1, tq), 1)
    cend = lax.broadcasted_iota(jnp.int32, (nc, 1), 0) * CMP_STRIDE + (CMP_BLOCK - 1)
    cmask = cend <= t
    any_vis = jnp.where(t >= CMP_BLOCK - 1, 1.0, 0.0)
    gate_t = gl_ref[...].astype(F32).T
    s_all = _dot(kc_ref[...], qt_ref[...])
    vct = vct_ref[...]
    psum = jnp.zeros((nc, tq), F32)
    for g in range(NSA_GROUP):
        s = jnp.where(cmask, s_all[:, g * tq:(g + 1) * tq], NEG_INF)
        e = jnp.exp2(s - jnp.max(s, axis=0, keepdims=True))
        p = e * (any_vis / jnp.sum(e, axis=0, keepdims=True))
        psum = psum + p
        _store_gated_t(_dot(vct, p.astype(BF16)), g, gate_t, o_ref, 0)

    js = lax.broadcasted_iota(jnp.int32, (n_sel, nc), 0) * SEL_BLOCK
    cs = lax.broadcasted_iota(jnp.int32, (n_sel, nc), 1) * CMP_STRIDE
    agg_t = jnp.where((cs < js + SEL_BLOCK) & (cs + CMP_BLOCK - 1 >= js), 1.0, 0.0).astype(BF16)
    hi, mid, lo = _split3(psum)
    imp = _dot(agg_t, hi) + _dot(agg_t, mid) + _dot(agg_t, lo)
    j = lax.broadcasted_iota(jnp.int32, (n_sel, 1), 0)
    bt = t // SEL_BLOCK
    forced = (j == 0) | (j == bt) | (j == bt - 1)
    sc = jnp.where(forced, SEL_BIG, jnp.where(j <= bt, imp, -SEL_BIG))
    rowi = lax.broadcasted_iota(jnp.int32, (n_sel, tq), 0).astype(F32)
    sel = jnp.zeros((n_sel, tq), F32)
    for _ in range(min(SEL_TOPK, n_sel)):
        mx = jnp.max(sc, axis=0, keepdims=True)
        idx = jnp.min(jnp.where(sc == mx, rowi, float(n_sel)), axis=0, keepdims=True)
        pick = rowi == idx
        sel = jnp.where(pick, 1.0, sel)
        sc = jnp.where(pick, -jnp.inf, sc)
    selt_ref[...] = sel


def nsa_cmp_attn(q_t, kc, vct, proj, *, batch, seq, gl_off, tq):
    m = batch * seq
    nq = seq // tq
    n_cmp = kc.shape[2]
    n_sel = seq // SEL_BLOCK
    d = NSA_HEAD_DIM
    gw = NSA_GROUP * d
    glb = gl_off // LANES
    return pl.pallas_call(
        functools.partial(_nsa_cmp_kernel, tq=tq, n_sel=n_sel),
        grid=(batch, NSA_KV_HEADS, nq),
        in_specs=[pl.BlockSpec((None, None, d, NSA_GROUP * tq), lambda b, h, i: (b * nq + i, h, 0, 0)),
                  pl.BlockSpec((None, None, n_cmp, d), lambda b, h, i: (b, h, 0, 0)),
                  pl.BlockSpec((None, None, d, n_cmp), lambda b, h, i: (b, h, 0, 0)),
                  pl.BlockSpec((tq, LANES), lambda b, h, i: (b * nq + i, glb + h))],
        out_specs=[pl.BlockSpec((tq, gw), lambda b, h, i: (b * nq + i, h)),
                   pl.BlockSpec((None, None, n_sel, tq), lambda b, h, i: (b, h, 0, i))],
        out_shape=[jax.ShapeDtypeStruct((m, W_TOK), BF16),
                   jax.ShapeDtypeStruct((batch, NSA_KV_HEADS, n_sel, seq), F32)],
        compiler_params=_cparams(("parallel", "parallel", "parallel")),
        name="nsa_cmp_attn",
    )(q_t, kc, vct, proj)


def _nsa_sel_kernel(qi_ref, ki_ref, first_ref, last_ref, qt_ref, k_ref, vt_ref, selt_ref, gl_ref, o_ref,
                    m_ref, acc_ref, *, tq, tk, nsub):
    step = pl.program_id(2)
    qi = qi_ref[step]
    ki = ki_ref[step]

    @pl.when(first_ref[step] == 1)
    def _():
        m_ref[...] = jnp.full_like(m_ref, NEG_INF)
        acc_ref[...] = jnp.zeros_like(acc_ref)

    kp = ki * tk + lax.broadcasted_iota(jnp.int32, (tk, 1), 0)
    per_tile = tk // SEL_BLOCK
    per_load = 8 // per_tile
    rows8 = selt_ref[pl.ds(pl.multiple_of((ki // per_load) * 8, 8), 8), :]
    sub = ki % per_load
    rows = rows8[0:per_tile]
    for u in range(1, per_load):
        rows = jnp.where(sub == u, rows8[u * per_tile:(u + 1) * per_tile], rows)
    k = k_ref[...]
    vt = jnp.concatenate([vt_ref[...], jnp.ones((ROWSUM_ROWS, tk), BF16)], axis=0)
    m_all = m_ref[...]
    m_rows = []
    for a in range(nsub):
        t = (qi * nsub + a) * tq + lax.broadcasted_iota(jnp.int32, (1, tq), 1)
        selm = jnp.concatenate([jnp.broadcast_to(rows[u:u + 1, a * tq:(a + 1) * tq], (SEL_BLOCK, tq))
                                for u in range(per_tile)], axis=0)
        mask = (selm > 0.5) & (kp <= t)
        bias = jnp.where(mask, 0.0, -jnp.inf)
        s_all = _dot(k, qt_ref[a]) + jnp.concatenate([bias] * NSA_GROUP, axis=1)
        for g in range(NSA_GROUP):
            idx = a * NSA_GROUP + g
            cols = slice(idx * tq, (idx + 1) * tq)
            s = s_all[:, g * tq:(g + 1) * tq]
            m_old = m_all[idx:idx + 1, :]
            m_new = jnp.maximum(m_old, jnp.max(s, axis=0, keepdims=True))
            alpha = jnp.exp2(m_old - m_new)
            p = jnp.exp2((s - m_new).astype(BF16))
            m_rows.append(m_new)
            acc_ref[:, cols] = alpha * acc_ref[:, cols] + _dot(vt, p)
    m_ref[...] = jnp.concatenate(m_rows, axis=0)

    @pl.when(last_ref[step] == 1)
    def _():
        gate_t = gl_ref[...].astype(F32).T
        d = NSA_HEAD_DIM
        for a in range(nsub):
            for g in range(NSA_GROUP):
                idx = a * NSA_GROUP + g
                cols = slice(idx * tq, (idx + 1) * tq)
                o_t = acc_ref[0:d, cols] / acc_ref[d:d + 1, cols]
                _store_gated_t(o_t, g, gate_t[:, a * tq:(a + 1) * tq], o_ref, 1, rows=slice(a * tq, (a + 1) * tq))


def nsa_sel_attn(q_t, k, v_t, sel_t, proj, *, batch, seq, gl_off, tq, tk, nsub):
    m = batch * seq
    tqs = nsub * tq
    nq = seq // tqs
    nk = seq // tk
    d = NSA_HEAD_DIM
    gw = NSA_GROUP * d
    glb = gl_off // LANES
    n_sel = seq // SEL_BLOCK
    assert 8 % (tk // SEL_BLOCK) == 0 and n_sel % 8 == 0 and seq % tqs == 0
    steps = []
    for i in range(nq):
        hi_k = ((i + 1) * tqs - 1) // tk
        for kk in range(hi_k + 1):
            steps.append((i, kk, int(kk == 0), int(kk == hi_k)))
    tabs = [jnp.asarray(np.array([s_[c] for s_ in steps], np.int32)) for c in range(4)]
    q_tile = pl.BlockSpec((tqs, gw), lambda b, h, s, qi, ki, fi, la: (b * nq + qi[s], h))
    in_specs = [pl.BlockSpec((nsub, None, d, NSA_GROUP * tq), lambda b, h, s, qi, ki, fi, la: (b * nq + qi[s], h, 0, 0)),
                pl.BlockSpec((tk, d), lambda b, h, s, qi, ki, fi, la: (b * nk + ki[s], h)),
                pl.BlockSpec((None, d, tk), lambda b, h, s, qi, ki, fi, la: (h, 0, b * nk + ki[s])),
                pl.BlockSpec((None, None, n_sel, tqs), lambda b, h, s, qi, ki, fi, la: (b, h, 0, qi[s])),
                pl.BlockSpec((tqs, LANES), lambda b, h, s, qi, ki, fi, la: (b * nq + qi[s], glb + h))]
    grid_spec = pltpu.PrefetchScalarGridSpec(
        num_scalar_prefetch=4,
        grid=(batch, NSA_KV_HEADS, len(steps)),
        in_specs=in_specs,
        out_specs=q_tile,
        scratch_shapes=[pltpu.VMEM((nsub * NSA_GROUP, tq), F32),
                        pltpu.VMEM((d + ROWSUM_ROWS, nsub * NSA_GROUP * tq), F32)])
    return pl.pallas_call(
        functools.partial(_nsa_sel_kernel, tq=tq, tk=tk, nsub=nsub),
        grid_spec=grid_spec,
        out_shape=jax.ShapeDtypeStruct((m, W_TOK), BF16),
        compiler_params=_cparams(("parallel", "parallel", "arbitrary")),
        name="nsa_sel_attn",
    )(*tabs, q_t, k, v_t, sel_t, proj)


def _nsa_win_kernel(qt_ref, *rest, tq, n_tiles):
    k_refs = rest[:n_tiles]
    vt_refs = rest[n_tiles:2 * n_tiles]
    gl_ref, ocmp_ref, osel_ref, out_gate_ref, o_ref = rest[2 * n_tiles:]
    i = pl.program_id(2)
    nk = n_tiles * tq
    t = i * tq + lax.broadcasted_iota(jnp.int32, (1, tq), 1)
    kp = (i - (n_tiles - 1)) * tq + lax.broadcasted_iota(jnp.int32, (nk, 1), 0)
    mask = (kp >= 0) & (kp <= t) & (kp > t - WINDOW)
    bias = jnp.where(mask, 0.0, -jnp.inf)
    k = jnp.concatenate([r[...] for r in k_refs], axis=0)
    vt = jnp.concatenate([r[...] for r in vt_refs], axis=1)
    vt = jnp.concatenate([vt, jnp.ones((ROWSUM_ROWS, nk), BF16)], axis=0)
    s_all = _dot(k, qt_ref[...]) + jnp.concatenate([bias] * NSA_GROUP, axis=1)
    gate_t = gl_ref[...].astype(F32).T
    d = NSA_HEAD_DIM
    for g in range(NSA_GROUP):
        s = s_all[:, g * tq:(g + 1) * tq]
        p = jnp.exp2((s - jnp.max(s, axis=0, keepdims=True)).astype(BF16))
        acc = _dot(vt, p)
        _store_gated_t(acc[0:d] / acc[d:d + 1], g, gate_t, o_ref, 2, (ocmp_ref, osel_ref), out_gate_ref)


def nsa_win_attn(q_t, k, v_t, proj, o_cmp, o_sel, *, batch, seq, gl_off, gate_off, tq):
    m = batch * seq
    nq = seq // tq
    d = NSA_HEAD_DIM
    gw = NSA_GROUP * d
    glb = gl_off // LANES
    assert WINDOW % tq == 0 and gate_off % gw == 0
    n_tiles = WINDOW // tq + 1
    key_tile = lambda u: (lambda b, h, i: b * nq + jnp.maximum(i - (n_tiles - 1) + u, 0))
    q_tile = lambda col: pl.BlockSpec((tq, gw), lambda b, h, i: (b * nq + i, col + h))
    in_specs = ([pl.BlockSpec((None, None, d, NSA_GROUP * tq), lambda b, h, i: (b * nq + i, h, 0, 0))]
                + [pl.BlockSpec((tq, d), lambda b, h, i, f=key_tile(u): (f(b, h, i), h)) for u in range(n_tiles)]
                + [pl.BlockSpec((None, d, tq), lambda b, h, i, f=key_tile(u): (h, 0, f(b, h, i))) for u in range(n_tiles)]
                + [pl.BlockSpec((tq, LANES), lambda b, h, i: (b * nq + i, glb + h)),
                   q_tile(0), q_tile(0), q_tile(gate_off // gw)])
    return pl.pallas_call(
        functools.partial(_nsa_win_kernel, tq=tq, n_tiles=n_tiles),
        grid=(batch, NSA_KV_HEADS, nq),
        in_specs=in_specs,
        out_specs=q_tile(0),
        out_shape=jax.ShapeDtypeStruct((m, W_TOK), BF16),
        compiler_params=_cparams(("parallel", "parallel", "parallel")),
        name="nsa_win_attn",
    )(q_t, *([k] * n_tiles), *([v_t] * n_tiles), proj, o_cmp, o_sel, proj)


def _nsa_layout(w_in):
    kv_end = W_TOK + 6 * NSA_KV_W
    gl_end = kv_end + 3 * NSA_HEADS
    k = w_in.shape[0]
    glw = w_in[:, kv_end:gl_end].reshape(k, NSA_KV_HEADS, NSA_GROUP, 3)
    glw = glw.transpose(0, 1, 3, 2).reshape(k, NSA_KV_HEADS, 3 * NSA_GROUP)
    glw = jnp.pad(glw, ((0, 0), (0, 0), (0, LANES - 3 * NSA_GROUP))).reshape(k, NSA_KV_HEADS * LANES)
    pad_w = 2 * W_TOK - kv_end - NSA_KV_HEADS * LANES
    w = jnp.concatenate([w_in[:, :kv_end], glw, jnp.zeros((k, pad_w), w_in.dtype), w_in[:, gl_end:]], axis=1)
    offs = {"q": 0, "kv": W_TOK, "gl": kv_end, "gate": 2 * W_TOK, "memq": 3 * W_TOK}
    return w.astype(BF16), offs


def _ssd_layout(w_in):
    k = w_in.shape[0]
    xbc_end = W_TOK + W_TOK + 2 * SSM_GROUPS * SSM_STATE
    dt_end = xbc_end + SSM_HEADS
    dtw = w_in[:, xbc_end:dt_end].reshape(k, SSM_GROUPS, SSM_RANK)
    dtw = jnp.pad(dtw, ((0, 0), (0, 0), (0, LANES - SSM_RANK))).reshape(k, SSM_GROUPS * LANES)
    w = jnp.concatenate([w_in[:, :xbc_end], w_in[:, dt_end:], dtw], axis=1)
    offs = {"z": 0, "xs": W_TOK, "B": 2 * W_TOK, "C": 2 * W_TOK + SSM_GROUPS * SSM_STATE,
            "memq": xbc_end, "dt": xbc_end + 2 * W_MEM}
    return w.astype(BF16), offs


def nsa_mixer(proj, positions, qn, kn, cmp_pos, kw1, kw2, vw1, vw2, *, batch, seq, offs):
    cos, sin = rope_tables(positions, tm=min(1024, batch * seq))
    tq = 256
    common = dict(batch=batch, seq=seq, gl_off=offs["gl"])
    q_t, cmp_in, k_s, vt_s, k_w, vt_w = nsa_prep(proj, cos, sin, qn, kn, offs=offs, tm=tq)
    kc, vct = nsa_compress(cmp_in, cmp_pos, kw1, kw2, vw1, vw2, batch=batch, seq=seq)
    o_cmp, sel_t = nsa_cmp_attn(q_t, kc, vct, proj, tq=tq, **common)
    o_sel = nsa_sel_attn(q_t, k_s, vt_s, sel_t, proj, tq=tq, tk=min(512, seq), nsub=2, **common)
    return nsa_win_attn(q_t, k_w, vt_w, proj, o_cmp, o_sel, gate_off=offs["gate"], tq=tq, **common)


def _layer(x2, mem2, mem_norm, norm, w_in_b, w_out, mem_wkv, mem_qn, mem_kn, mixer, memq_off, *, seq, gnorm=None):
    m = x2.shape[0]
    proj = norm_matmul(x2, norm, w_in_b, tm=_tile(m, 1024), tn=_tile(w_in_b.shape[1], 1024), out_dtype=BF16)
    kv = norm_matmul(mem2, mem_norm, mem_wkv.astype(BF16), tm=_tile(mem2.shape[0], 512), tn=512, out_dtype=F32)
    y_mem = mem_attn(proj, kv, mem_qn, mem_kn, q_off=memq_off, seq=seq, tq=min(512, seq))
    y_tok = mixer(proj)
    w_tok = w_out[:W_TOK].astype(BF16)
    w_mem = w_out[W_TOK:].astype(BF16)
    norm_tok = gnorm is not None
    gain = gnorm if norm_tok else jnp.ones((W_TOK,), F32)
    tm, tn = (_tile(m, 512), 1024) if norm_tok else (_tile(m, 1024), 512)
    return out_proj(y_tok, y_mem, w_tok, w_mem, x2, gain, norm_tok=norm_tok, tm=tm, tn=tn)


def kernel(x, mem, positions, mem_norm, l0_norm, l0_w_in, l0_w_out, l0_mem_wkv, l0_mem_qnorm, l0_mem_knorm, l0_pool_w, l0_pool_scale, l1_norm, l1_w_in, l1_w_out, l1_mem_wkv, l1_mem_qnorm, l1_mem_knorm, l1_qnorm, l1_knorm, l1_cmp_pos, l1_cmp_k_w1, l1_cmp_k_w2, l1_cmp_v_w1, l1_cmp_v_w2, l2_norm, l2_w_in, l2_w_out, l2_mem_wkv, l2_mem_qnorm, l2_mem_knorm, l2_conv_w, l2_conv_b, l2_dt_bias, l2_A_log, l2_D, l2_gnorm, l3_norm, l3_w_in, l3_w_out, l3_mem_wkv, l3_mem_qnorm, l3_mem_knorm, l3_v_norm, l3_sgu_w, l3_sgu_b):
    batch, seq, d = x.shape
    x2 = x.reshape(batch * seq, d)
    mem2 = mem.reshape(batch * MEM_TOKENS, d)

    x2 = _layer(x2, mem2, mem_norm, l0_norm, l0_w_in.astype(BF16), l0_w_out, l0_mem_wkv, l0_mem_qnorm, l0_mem_knorm,
                lambda p: pool_mixer(p, l0_pool_w, l0_pool_scale, seq=seq, tm=256), 2 * W_TOK, seq=seq)

    w1, offs1 = _nsa_layout(l1_w_in)
    x2 = _layer(x2, mem2, mem_norm, l1_norm, w1, l1_w_out, l1_mem_wkv, l1_mem_qnorm, l1_mem_knorm,
                lambda p: nsa_mixer(p, positions, l1_qnorm, l1_knorm, l1_cmp_pos, l1_cmp_k_w1, l1_cmp_k_w2,
                                    l1_cmp_v_w1, l1_cmp_v_w2, batch=batch, seq=seq, offs=offs1),
                offs1["memq"], seq=seq)

    w2, offs2 = _ssd_layout(l2_w_in)
    x2 = _layer(x2, mem2, mem_norm, l2_norm, w2, l2_w_out, l2_mem_wkv, l2_mem_qnorm, l2_mem_knorm,
                lambda p: ssd_mixer(p, l2_conv_w, l2_conv_b, l2_dt_bias, l2_A_log, l2_D, batch=batch, seq=seq, offs=offs2),
                offs2["memq"], seq=seq, gnorm=l2_gnorm)

    x2 = _layer(x2, mem2, mem_norm, l3_norm, l3_w_in.astype(BF16), l3_w_out, l3_mem_wkv, l3_mem_qnorm, l3_mem_knorm,
                lambda p: sgu_mixer(p, l3_v_norm, l3_sgu_w, l3_sgu_b, tm=256), 3 * W_TOK, seq=seq)
    return x2.reshape(batch, seq, d)
```
